```python
import jax, jax.numpy as jnp
from jax import lax
import numpy as np

D_MODEL = 2048
BATCH = 4
SEQ = 2048
DEPTH = 1

CHUNK = 64
PLE_DIM = 256
EPS = 1e-6

MIX_A = D_MODEL // 2
CONV_GROUPS = 8
CONV_WIDTH = 3

MIX_B = D_MODEL // 2
GLA_HEADS = 4
GLA_DK_TOTAL = MIX_B // 2
GLA_DK = GLA_DK_TOTAL // GLA_HEADS
GLA_DV = MIX_B // GLA_HEADS
GLA_GATE_RANK = 16
GLA_TAU = 16.0

D_FF = ((8 * D_MODEL // 3 + 255) // 256) * 256

IN_SPLITS = (MIX_A, MIX_A, MIX_A,
             GLA_DK_TOTAL, GLA_DK_TOTAL, MIX_B, MIX_B, GLA_GATE_RANK,
             D_MODEL, D_MODEL)
IN_COLS = sum(IN_SPLITS)

kernel_name = "hybrid_shortconv_gla_gated_merge_block"


def rms_norm(x, gain):
    xf = x.astype(jnp.float32)
    y = xf * lax.rsqrt(jnp.mean(xf * xf, axis=-1, keepdims=True) + EPS)
    return (y * gain.astype(jnp.float32)).astype(x.dtype)


def short_conv_branch(u_x, u_b, u_c, conv_w, w_out):
    u = u_c * u_x
    y = lax.conv_general_dilated(
        u, conv_w[:, None, :].astype(u.dtype), window_strides=(1,),
        padding=[(CONV_WIDTH - 1, 0)],
        dimension_numbers=('NWC', 'WIO', 'NWC'),
        feature_group_count=MIX_A)
    return (u_b * y) @ w_out


def gla_branch(q, k, v, og, a_lr, w_alpha_up, b_alpha_up, head_gain, w_out):
    out_dtype = v.dtype
    bsz, seq = q.shape[0], q.shape[1]
    n = seq // CHUNK
    f32 = jnp.float32
    z = a_lr.astype(f32) @ w_alpha_up.astype(f32) + b_alpha_up.astype(f32)
    log_a = jax.nn.log_sigmoid(z) / GLA_TAU
    shp_k = (bsz, n, CHUNK, GLA_HEADS, GLA_DK)
    q = q.astype(f32).reshape(shp_k) * (GLA_DK ** -0.5)
    k = k.astype(f32).reshape(shp_k)
    v = v.astype(f32).reshape(bsz, n, CHUNK, GLA_HEADS, GLA_DV)
    b = jnp.cumsum(log_a.reshape(shp_k), axis=2)
    b_last = b[:, :, -1:]
    mid = b[:, :, CHUNK // 2:CHUNK // 2 + 1]
    a_fwd = jnp.einsum('bnthd,bnshd->bnhts', q * jnp.exp(b - mid), k * jnp.exp(mid - b))
    a_rev = jnp.einsum('bnthd,bnshd->bnhts', q * jnp.exp(mid - b), k * jnp.exp(b - mid))
    lower = jnp.tril(jnp.ones((CHUNK, CHUNK), dtype=bool))
    att = jnp.where(lower, a_fwd, a_rev)
    o_intra = jnp.einsum('bnhts,bnshv->bnthv', att, v)
    u_c = jnp.einsum('bnshd,bnshv->bnhdv', k * jnp.exp(b_last - b), v)
    decay = jnp.exp(b_last[:, :, 0])

    def step(state, inp):
        dec, upd = inp
        return dec[..., None] * state + upd, state

    s0 = jnp.zeros((bsz, GLA_HEADS, GLA_DK, GLA_DV), f32)
    _, s_in = lax.scan(step, s0, (jnp.moveaxis(decay, 1, 0), jnp.moveaxis(u_c, 1, 0)))
    s_in = jnp.moveaxis(s_in, 0, 1)
    o_inter = jnp.einsum('bnthd,bnhdv->bnthv', q * jnp.exp(b), s_in)
    o = (o_intra + o_inter).reshape(bsz, seq, GLA_HEADS, GLA_DV)
    o = o * lax.rsqrt(jnp.mean(o * o, axis=-1, keepdims=True) + EPS) * head_gain.astype(f32)
    o = o.reshape(bsz, seq, MIX_B) * jax.nn.silu(og.astype(f32))
    return o.astype(out_dtype) @ w_out


def setup_inputs(seed: int = 0) -> dict:
    key = jax.random.key(seed)
    ks = jax.random.split(key, 24)
    f32 = jnp.float32

    def nrm(k, shape, scale):
        return jax.random.normal(k, shape, f32) * scale

    def gain(k, dim):
        return 1.0 + 0.05 * jax.random.normal(k, (DEPTH, dim), f32)

    return {
        "x": nrm(ks[0], (BATCH, SEQ, D_MODEL), 1.0),
        "p": nrm(ks[1], (DEPTH, BATCH, SEQ, PLE_DIM), 1.0),
        "w_in": nrm(ks[2], (DEPTH, D_MODEL, IN_COLS), D_MODEL ** -0.5),
        "conv_w": nrm(ks[3], (DEPTH, CONV_WIDTH, MIX_A), CONV_WIDTH ** -0.5),
        "w_a_out": nrm(ks[4], (DEPTH, MIX_A, D_MODEL), MIX_A ** -0.5),
        "w_alpha_up": nrm(ks[5], (DEPTH, GLA_GATE_RANK, GLA_DK_TOTAL), GLA_GATE_RANK ** -0.5),
        "b_alpha_up": nrm(ks[6], (DEPTH, GLA_DK_TOTAL), 0.1),
        "gla_head_gain": gain(ks[7], GLA_DV),
        "w_b_out": nrm(ks[8], (DEPTH, MIX_B, D_MODEL), MIX_B ** -0.5),
        "w_mix_out": nrm(ks[9], (DEPTH, D_MODEL, D_MODEL), D_MODEL ** -0.5),
        "g_pre_mix": gain(ks[10], D_MODEL),
        "g_post_mix": gain(ks[11], D_MODEL),
        "g_pre_ffn": gain(ks[12], D_MODEL),
        "g_post_ffn": gain(ks[13], D_MODEL),
        "w_ff_gate": nrm(ks[14], (DEPTH, D_MODEL, D_FF), D_MODEL ** -0.5),
        "w_ff_up": nrm(ks[15], (DEPTH, D_MODEL, D_FF), D_MODEL ** -0.5),
        "w_ff_down": nrm(ks[16], (DEPTH, D_FF, D_MODEL), D_FF ** -0.5),
        "g_pre_ple": gain(ks[17], D_MODEL),
        "g_post_ple": gain(ks[18], D_MODEL),
        "w_ple_gate": nrm(ks[19], (DEPTH, D_MODEL, D_MODEL), D_MODEL ** -0.5),
        "w_ple_proj": nrm(ks[20], (DEPTH, PLE_DIM, D_MODEL), PLE_DIM ** -0.5),
    }


def reference(x, p, w_in, conv_w, w_a_out, w_alpha_up, b_alpha_up, gla_head_gain,
              w_b_out, w_mix_out, g_pre_mix, g_post_mix, g_pre_ffn, g_post_ffn,
              w_ff_gate, w_ff_up, w_ff_down, g_pre_ple, g_post_ple,
              w_ple_gate, w_ple_proj):
    split_idx = [int(c) for c in np.cumsum(IN_SPLITS)[:-1]]
    for i in range(DEPTH):
        h = rms_norm(x, g_pre_mix[i])
        proj = h @ w_in[i]
        (a_x, a_b, a_c, q, k, v, og, a_lr, gate_a, gate_b) = jnp.split(proj, split_idx, axis=-1)
        y_a = short_conv_branch(a_x, a_b, a_c, conv_w[i], w_a_out[i])
        y_b = gla_branch(q, k, v, og, a_lr, w_alpha_up[i], b_alpha_up[i],
                         gla_head_gain[i], w_b_out[i])
        mix = jax.nn.sigmoid(gate_a) * y_a + jax.nn.sigmoid(gate_b) * y_b
        x = x + rms_norm(mix @ w_mix_out[i], g_post_mix[i])
        h = rms_norm(x, g_pre_ffn[i])
        f = (jax.nn.silu(h @ w_ff_gate[i]) * (h @ w_ff_up[i])) @ w_ff_down[i]
        x = x + rms_norm(f, g_post_ffn[i])
        h = rms_norm(x, g_pre_ple[i])
        e = jax.nn.sigmoid(h @ w_ple_gate[i]) * (p[i] @ w_ple_proj[i])
        x = x + rms_norm(e, g_post_ple[i])
    return x
```

```python
import functools

import jax
import jax.numpy as jnp
from jax import lax
from jax.experimental import pallas as pl
from jax.experimental.pallas import tpu as pltpu

EPS = 1e-6
CHUNK = 64
GLA_TAU = 16.0
CONV_WIDTH = 3
LANES = 128
SUBLANES = 8
VMEM_LIMIT_CAP = 60000 * 1024

F32 = jnp.float32
BF16 = jnp.bfloat16


def _rms(x, gain):
    ms = jnp.mean(x * x, axis=-1, keepdims=True)
    return x * lax.rsqrt(ms + EPS) * gain


def _dot(a, b):
    return jnp.dot(a, b, preferred_element_type=F32)


def _dot_nt(a, b):
    return lax.dot_general(a, b, (((1,), (1,)), ((), ())), preferred_element_type=F32)


def _dot_tn(a, b):
    return lax.dot_general(a, b, (((0,), (0,)), ((), ())), preferred_element_type=F32)


def _params(semantics, vmem_bytes):
    return pltpu.CompilerParams(
        dimension_semantics=semantics,
        vmem_limit_bytes=min(int(vmem_bytes), VMEM_LIMIT_CAP),
    )


def _resident(shape):
    return pl.BlockSpec(shape, lambda *_: (0,) * len(shape), pipeline_mode=pl.Buffered(1))


def _norm_proj_kernel(x_ref, g_ref, w_ref, o_ref, h_ref):
    @pl.when(pl.program_id(1) == 0)
    def _():
        h_ref[...] = _rms(x_ref[...], g_ref[...]).astype(BF16)

    o_ref[...] = _dot(h_ref[...], w_ref[...]).astype(o_ref.dtype)


def _norm_proj(x2, gain, w, *, bm, bn):
    m, d = x2.shape
    n = w.shape[1]
    vmem = 2 * bm * d * 4 + 2 * d * bn * 2 + 2 * bm * bn * 2 + bm * d * 2 + bm * bn * 4
    return pl.pallas_call(
        _norm_proj_kernel,
        out_shape=jax.ShapeDtypeStruct((m, n), BF16),
        grid=(m // bm, n // bn),
        in_specs=[
            pl.BlockSpec((bm, d), lambda i, j: (i, 0)),
            pl.BlockSpec((1, d), lambda i, j: (0, 0)),
            pl.BlockSpec((d, bn), lambda i, j: (0, j)),
        ],
        out_specs=pl.BlockSpec((bm, bn), lambda i, j: (i, j)),
        scratch_shapes=[pltpu.VMEM((bm, d), BF16)],
        compiler_params=_params(("parallel", "arbitrary"), vmem + (8 << 20)),
        name="norm_proj",
    )(x2, gain, w)


def _gla_kernel(q_ref, k_ref, v_ref, og_ref, alr_ref, wup_ref, bup_ref, hg_ref,
                o_ref, state_ref, *, heads, dk, dv, ts):
    @pl.when(pl.program_id(1) == 0)
    def _():
        state_ref[...] = jnp.zeros_like(state_ref)

    z = _dot(alr_ref[...], wup_ref[...]) + bup_ref[...]
    log_a = (jnp.minimum(z, 0.0) - jnp.log1p(jnp.exp(-jnp.abs(z)))) * (1.0 / GLA_TAU)

    row = lax.broadcasted_iota(jnp.int32, (ts, ts), 0)
    col = lax.broadcasted_iota(jnp.int32, (ts, ts), 1)
    tri = ((row // CHUNK == col // CHUNK) & (col <= row)).astype(BF16)
    la_hi = log_a.astype(BF16)
    la_lo = (log_a - la_hi.astype(F32)).astype(BF16)
    b = _dot(tri, la_hi) + _dot(tri, la_lo)

    crow = lax.broadcasted_iota(jnp.int32, (CHUNK, CHUNK), 0)
    ccol = lax.broadcasted_iota(jnp.int32, (CHUNK, CHUNK), 1)
    lower = ccol <= crow

    scale = dk ** -0.5
    gain = hg_ref[...]
    states = [state_ref[h] for h in range(heads)]
    for c in range(ts // CHUNK):
        r = slice(c * CHUNK, (c + 1) * CHUNK)
        bc = b[r, :]
        mid = bc[CHUNK // 2:CHUNK // 2 + 1, :]
        last = bc[CHUNK - 1:CHUNK, :]
        e_fwd = jnp.exp(bc - mid)
        e_rev = jnp.exp(mid - bc)
        qc = q_ref[r, :].astype(F32) * scale
        kc = k_ref[r, :].astype(F32)
        q_f = (qc * e_fwd).astype(BF16)
        k_f = (kc * e_rev).astype(BF16)
        q_r = (qc * e_rev).astype(BF16)
        k_r = (kc * e_fwd).astype(BF16)
        k_l = (kc * jnp.exp(last - bc)).astype(BF16)
        q_i = (qc * jnp.exp(bc)).astype(BF16)
        decay = jnp.exp(last)
        for h in range(heads):
            ck = slice(h * dk, (h + 1) * dk)
            cv = slice(h * dv, (h + 1) * dv)
            vh = v_ref[r, cv]
            att = jnp.where(lower, _dot_nt(q_f[:, ck], k_f[:, ck]), _dot_nt(q_r[:, ck], k_r[:, ck]))
            st = states[h]
            o = _dot(att.astype(BF16), vh) + _dot_nt(q_i[:, ck], st.astype(BF16))
            states[h] = st * decay[:, ck] + _dot_tn(vh, k_l[:, ck])
            o = o * lax.rsqrt(jnp.mean(o * o, axis=-1, keepdims=True) + EPS) * gain
            og = og_ref[r, cv].astype(F32)
            o_ref[r, cv] = (o * (og * jax.nn.sigmoid(og))).astype(o_ref.dtype)
    for h in range(heads):
        state_ref[h] = states[h]


def _gla(proj, w_up, b_up, head_gain, *, batch, seq, heads, dk, dv, cols, ts):
    m = proj.shape[0]
    ns = seq // ts
    dkt, dvt = heads * dk, heads * dv
    rank_pad = w_up.shape[0]

    def rows(b, s):
        return b * ns + s

    kernel = functools.partial(_gla_kernel, heads=heads, dk=dk, dv=dv, ts=ts)
    vmem = 2 * ts * (2 * dkt + 2 * dvt + rank_pad) * 2 + 2 * ts * dvt * 2 + heads * dk * dv * 4
    return pl.pallas_call(
        kernel,
        out_shape=jax.ShapeDtypeStruct((m, dvt), BF16),
        grid=(batch, ns),
        in_specs=[
            pl.BlockSpec((ts, dkt), lambda b, s: (rows(b, s), cols["q"] // dkt)),
            pl.BlockSpec((ts, dkt), lambda b, s: (rows(b, s), cols["k"] // dkt)),
            pl.BlockSpec((ts, dvt), lambda b, s: (rows(b, s), cols["v"] // dvt)),
            pl.BlockSpec((ts, dvt), lambda b, s: (rows(b, s), cols["og"] // dvt)),
            pl.BlockSpec((ts, rank_pad), lambda b, s: (rows(b, s), cols["a_lr"] // rank_pad)),
            _resident(w_up.shape),
            _resident(b_up.shape),
            _resident(head_gain.shape),
        ],
        out_specs=pl.BlockSpec((ts, dvt), lambda b, s: (rows(b, s), 0)),
        scratch_shapes=[pltpu.VMEM((heads, dv, dk), F32)],
        compiler_params=_params(("parallel", "arbitrary"), vmem + (24 << 20)),
        name="gla",
    )(proj, proj, proj, proj, proj, w_up, b_up, head_gain)


def _mix_out_kernel(ax_ref, ab_ref, ac_ref, axp_ref, acp_ref, o_ref, ga_ref, gb_ref, x_ref,
                    cw_ref, wa_ref, wb_ref, wm_ref, g_ref, out_ref, *, tiles_per_seq):
    u = ac_ref[...].astype(F32) * ax_ref[...].astype(F32)
    u_prev = acp_ref[...].astype(F32) * axp_ref[...].astype(F32)
    first = (pl.program_id(0) % tiles_per_seq) == 0
    u_prev = jnp.where(first, 0.0, u_prev)
    p1 = u_prev[SUBLANES - 1:SUBLANES, :]
    p2 = u_prev[SUBLANES - 2:SUBLANES - 1, :]
    row = lax.broadcasted_iota(jnp.int32, u.shape, 0)
    u1 = jnp.where(row == 0, p1, pltpu.roll(u, 1, 0))
    u2 = jnp.where(row == 0, p2, jnp.where(row == 1, p1, pltpu.roll(u, 2, 0)))
    cw = cw_ref[...]
    y = cw[0:1, :] * u2 + cw[1:2, :] * u1 + cw[2:3, :] * u
    ca = (ab_ref[...].astype(F32) * y).astype(BF16)

    ya = _dot(ca, wa_ref[...])
    yb = _dot(o_ref[...], wb_ref[...])
    mix = jax.nn.sigmoid(ga_ref[...].astype(F32)) * ya + jax.nn.sigmoid(gb_ref[...].astype(F32)) * yb
    m = _dot(mix.astype(BF16), wm_ref[...])
    out_ref[...] = x_ref[...] + _rms(m, g_ref[...])


def _mix_out(proj, o, x2, conv_w, wa, wb, wm, gain, *, seq, cols, bm):
    m, d = x2.shape
    ca = conv_w.shape[1]
    cb = o.shape[1]
    tiles_per_seq = seq // bm
    halo = bm // SUBLANES

    def prev(i):
        return jnp.maximum(i * halo - 1, 0)

    kernel = functools.partial(_mix_out_kernel, tiles_per_seq=tiles_per_seq)
    vmem = (2 * bm * (3 * ca + cb + 2 * d) * 2 + 4 * bm * d * 4
            + (ca + cb + d) * d * 2 + 4 * bm * d * 4)
    return pl.pallas_call(
        kernel,
        out_shape=jax.ShapeDtypeStruct((m, d), F32),
        grid=(m // bm,),
        in_specs=[
            pl.BlockSpec((bm, ca), lambda i: (i, cols["a_x"] // ca)),
            pl.BlockSpec((bm, ca), lambda i: (i, cols["a_b"] // ca)),
            pl.BlockSpec((bm, ca), lambda i: (i, cols["a_c"] // ca)),
            pl.BlockSpec((SUBLANES, ca), lambda i: (prev(i), cols["a_x"] // ca)),
            pl.BlockSpec((SUBLANES, ca), lambda i: (prev(i), cols["a_c"] // ca)),
            pl.BlockSpec((bm, cb), lambda i: (i, 0)),
            pl.BlockSpec((bm, d), lambda i: (i, cols["gate_a"] // d)),
            pl.BlockSpec((bm, d), lambda i: (i, cols["gate_b"] // d)),
            pl.BlockSpec((bm, d), lambda i: (i, 0)),
            _resident(conv_w.shape),
            _resident(wa.shape),
            _resident(wb.shape),
            _resident(wm.shape),
            _resident(gain.shape),
        ],
        out_specs=pl.BlockSpec((bm, d), lambda i: (i, 0)),
        compiler_params=_params(("arbitrary",), vmem + (8 << 20)),
        name="mix_out",
    )(proj, proj, proj, proj, proj, o, proj, proj, x2, conv_w, wa, wb, wm, gain)


def _ffn_kernel(x_ref, gpre_ref, wg_ref, wu_ref, wd_ref, gpost_ref, out_ref, h_ref, acc_ref):
    j = pl.program_id(1)

    @pl.when(j == 0)
    def _():
        h_ref[...] = _rms(x_ref[...], gpre_ref[...]).astype(BF16)
        acc_ref[...] = jnp.zeros_like(acc_ref)

    h = h_ref[...]
    g = _dot(h, wg_ref[...])
    u = _dot(h, wu_ref[...])
    a = (g * jax.nn.sigmoid(g) * u).astype(BF16)
    acc_ref[...] += _dot(a, wd_ref[...])

    @pl.when(j == pl.num_programs(1) - 1)
    def _():
        out_ref[...] = x_ref[...] + _rms(acc_ref[...], gpost_ref[...])


def _ffn(x2, g_pre, wg, wu, wd, g_post, *, bm, bf):
    m, d = x2.shape
    f = wg.shape[1]
    vmem = 4 * bm * d * 4 + 2 * 3 * d * bf * 2 + bm * d * 2 + bm * d * 4 + 3 * bm * bf * 4
    return pl.pallas_call(
        _ffn_kernel,
        out_shape=jax.ShapeDtypeStruct((m, d), F32),
        grid=(m // bm, f // bf),
        in_specs=[
            pl.BlockSpec((bm, d), lambda i, j: (i, 0)),
            pl.BlockSpec((1, d), lambda i, j: (0, 0)),
            pl.BlockSpec((d, bf), lambda i, j: (0, j)),
            pl.BlockSpec((d, bf), lambda i, j: (0, j)),
            pl.BlockSpec((bf, d), lambda i, j: (j, 0)),
            pl.BlockSpec((1, d), lambda i, j: (0, 0)),
        ],
        out_specs=pl.BlockSpec((bm, d), lambda i, j: (i, 0)),
        scratch_shapes=[pltpu.VMEM((bm, d), BF16), pltpu.VMEM((bm, d), F32)],
        compiler_params=_params(("parallel", "arbitrary"), vmem + (8 << 20)),
        name="ffn",
    )(x2, g_pre, wg, wu, wd, g_post)


def _ple_kernel(x_ref, p_ref, gpre_ref, wpg_ref, wpp_ref, gpost_ref, out_ref):
    x = x_ref[...]
    h = _rms(x, gpre_ref[...]).astype(BF16)
    e = jax.nn.sigmoid(_dot(h, wpg_ref[...])) * _dot(p_ref[...].astype(BF16), wpp_ref[...])
    out_ref[...] = x + _rms(e, gpost_ref[...])


def _ple(x2, p2, g_pre, wpg, wpp, g_post, *, bm):
    m, d = x2.shape
    pd = p2.shape[1]
    vmem = 4 * bm * d * 4 + 2 * bm * pd * 4 + (d + pd) * d * 2 + 4 * bm * d * 4
    return pl.pallas_call(
        _ple_kernel,
        out_shape=jax.ShapeDtypeStruct((m, d), F32),
        grid=(m // bm,),
        in_specs=[
            pl.BlockSpec((bm, d), lambda i: (i, 0)),
            pl.BlockSpec((bm, pd), lambda i: (i, 0)),
            _resident(g_pre.shape),
            _resident(wpg.shape),
            _resident(wpp.shape),
            _resident(g_post.shape),
        ],
        out_specs=pl.BlockSpec((bm, d), lambda i: (i, 0)),
        compiler_params=_params(("parallel",), vmem + (8 << 20)),
        name="ple",
    )(x2, p2, g_pre, wpg, wpp, g_post)


def _proj_layout(mix_a, dk_total, mix_b, rank, d):
    names = ["a_x", "a_b", "a_c", "q", "k", "v", "og", "gate_a", "gate_b", "a_lr"]
    widths = [mix_a, mix_a, mix_a, dk_total, dk_total, mix_b, mix_b, d, d, LANES]
    cols, off = {}, 0
    for nme, wdt in zip(names, widths):
        assert off % wdt == 0, (nme, off, wdt)
        cols[nme] = off
        off += wdt
    return cols, off


def kernel(x, p, w_in, conv_w, w_a_out, w_alpha_up, b_alpha_up, gla_head_gain, w_b_out, w_mix_out, g_pre_mix, g_post_mix, g_pre_ffn, g_post_ffn, w_ff_gate, w_ff_up, w_ff_down, g_pre_ple, g_post_ple, w_ple_gate, w_ple_proj):
    batch, seq, d = x.shape
    depth = w_in.shape[0]
    mix_a = conv_w.shape[-1]
    rank, dk_total = w_alpha_up.shape[1], w_alpha_up.shape[2]
    dv = gla_head_gain.shape[-1]
    mix_b = w_b_out.shape[1]
    heads = mix_b // dv
    dk = dk_total // heads
    assert rank <= LANES and seq % CHUNK == 0
    cols, n_proj = _proj_layout(mix_a, dk_total, mix_b, rank, d)
    lr0 = 3 * mix_a + 2 * dk_total + 2 * mix_b

    xs = x.reshape(batch * seq, d)
    for i in range(depth):
        wi = w_in[i]
        w_proj = jnp.concatenate(
            [wi[:, :lr0], wi[:, lr0 + rank:],
             jnp.pad(wi[:, lr0:lr0 + rank], ((0, 0), (0, LANES - rank)))], axis=1).astype(BF16)
        w_up = jnp.pad(w_alpha_up[i], ((0, LANES - rank), (0, 0))).astype(BF16)

        proj = _norm_proj(xs, g_pre_mix[i][None], w_proj, bm=1024, bn=n_proj // 9)
        o = _gla(proj, w_up, b_alpha_up[i][None], gla_head_gain[i][None],
                 batch=batch, seq=seq, heads=heads, dk=dk, dv=dv, cols=cols, ts=256)
        xs = _mix_out(proj, o, xs, conv_w[i], w_a_out[i].astype(BF16), w_b_out[i].astype(BF16),
                      w_mix_out[i].astype(BF16), g_post_mix[i][None], seq=seq, cols=cols, bm=256)
        xs = _ffn(xs, g_pre_ffn[i][None], w_ff_gate[i].astype(BF16), w_ff_up[i].astype(BF16),
                  w_ff_down[i].astype(BF16), g_post_ffn[i][None], bm=512, bf=512)
        xs = _ple(xs, p[i].reshape(batch * seq, -1), g_pre_ple[i][None], w_ple_gate[i].astype(BF16),
                  w_ple_proj[i].astype(BF16), g_post_ple[i][None], bm=512)
    return xs.reshape(batch, seq, d)
```

```python
import functools

import jax
import jax.numpy as jnp
from jax import lax
from jax.experimental import pallas as pl
from jax.experimental.pallas import tpu as pltpu

EPS = 1e-6
CHUNK = 64
GLA_TAU = 16.0
CONV_WIDTH = 3
LANES = 128
SUBLANES = 8
VMEM_LIMIT_CAP = 60000 * 1024

F32 = jnp.float32
BF16 = jnp.bfloat16


def _rms(x, gain):
    ms = jnp.mean(x * x, axis=-1, keepdims=True)
    return x * lax.rsqrt(ms + EPS) * gain


def _dot(a, b):
    return jnp.dot(a, b, preferred_element_type=F32)


def _dot_nt(a, b):
    return lax.dot_general(a, b, (((1,), (1,)), ((), ())), preferred_element_type=F32)


def _dot_tn(a, b):
    return lax.dot_general(a, b, (((0,), (0,)), ((), ())), preferred_element_type=F32)


def _params(semantics, vmem_bytes):
    return pltpu.CompilerParams(
        dimension_semantics=semantics,
        vmem_limit_bytes=min(int(vmem_bytes), VMEM_LIMIT_CAP),
    )


def _resident(shape):
    return pl.BlockSpec(shape, lambda *_: (0,) * len(shape), pipeline_mode=pl.Buffered(1))


def _norm_proj_kernel(x_ref, g_ref, wt_ref, wlr_ref, o_ref, lr_ref, h_ref):
    @pl.when(pl.program_id(1) == 0)
    def _():
        h = _rms(x_ref[...], g_ref[...]).astype(BF16)
        h_ref[...] = h
        wlr = wlr_ref[...]
        wlr = jnp.concatenate([wlr, jnp.zeros((LANES - wlr.shape[0], wlr.shape[1]), F32)], axis=0)
        lr_ref[...] = _dot_nt(h, wlr.astype(BF16)).astype(lr_ref.dtype)

    o_ref[...] = _dot_nt(h_ref[...], wt_ref[...].astype(BF16)).astype(o_ref.dtype)


def _norm_proj(x2, gain, wt, *, lr0, rank, bm, bn):
    m, d = x2.shape
    n = wt.shape[0] - rank
    assert lr0 % bn == 0 and n % bn == 0 and rank % SUBLANES == 0

    def w_rows(i, j):
        skip = jnp.where(j * bn >= lr0, rank // SUBLANES, 0)
        return ((j * (bn // SUBLANES) + skip) * SUBLANES, 0)

    vmem = 2 * bm * d * 4 + 2 * bn * d * 4 + 2 * bm * bn * 2 + bm * d * 2 + bm * bn * 4 + d * bn * 2
    return pl.pallas_call(
        _norm_proj_kernel,
        out_shape=(jax.ShapeDtypeStruct((m, n), BF16), jax.ShapeDtypeStruct((m, LANES), BF16)),
        grid=(m // bm, n // bn),
        in_specs=[
            pl.BlockSpec((bm, d), lambda i, j: (i, 0)),
            pl.BlockSpec((1, d), lambda i, j: (0, 0)),
            pl.BlockSpec((pl.Element(bn), pl.Element(d)), w_rows),
            pl.BlockSpec((pl.Element(rank), pl.Element(d)), lambda i, j: (lr0, 0)),
        ],
        out_specs=(pl.BlockSpec((bm, bn), lambda i, j: (i, j)),
                   pl.BlockSpec((bm, LANES), lambda i, j: (i, 0))),
        scratch_shapes=[pltpu.VMEM((bm, d), BF16)],
        compiler_params=_params(("parallel", "arbitrary"), vmem + (8 << 20)),
        name="norm_proj",
    )(x2, gain, wt, wt)


def _gla_kernel(q_ref, k_ref, v_ref, og_ref, alr_ref, wup_ref, bup_ref, hg_ref,
                o_ref, state_ref, *, heads, dk, dv, ts):
    @pl.when(pl.program_id(1) == 0)
    def _():
        state_ref[...] = jnp.zeros_like(state_ref)

    w_up = wup_ref[...]
    rank = w_up.shape[0]
    w_up = jnp.concatenate([w_up, jnp.zeros((LANES - rank, w_up.shape[1]), F32)], axis=0)
    z = _dot(alr_ref[...], w_up.astype(BF16)) + bup_ref[...]
    log_a = (jnp.minimum(z, 0.0) - jnp.log1p(jnp.exp(-jnp.abs(z)))) * (1.0 / GLA_TAU)

    row = lax.broadcasted_iota(jnp.int32, (ts, ts), 0)
    col = lax.broadcasted_iota(jnp.int32, (ts, ts), 1)
    tri = ((row // CHUNK == col // CHUNK) & (col <= row)).astype(BF16)
    la_hi = log_a.astype(BF16)
    la_lo = (log_a - la_hi.astype(F32)).astype(BF16)
    b = _dot(tri, la_hi) + _dot(tri, la_lo)

    crow = lax.broadcasted_iota(jnp.int32, (CHUNK, CHUNK), 0)
    ccol = lax.broadcasted_iota(jnp.int32, (CHUNK, CHUNK), 1)
    lower = ccol <= crow

    scale = dk ** -0.5
    gain = hg_ref[...]
    states = [state_ref[h] for h in range(heads)]
    for c in range(ts // CHUNK):
        r = slice(c * CHUNK, (c + 1) * CHUNK)
        bc = b[r, :]
        mid = bc[CHUNK // 2:CHUNK // 2 + 1, :]
        last = bc[CHUNK - 1:CHUNK, :]
        e_fwd = jnp.exp(bc - mid)
        e_rev = jnp.exp(mid - bc)
        qc = q_ref[r, :].astype(F32) * scale
        kc = k_ref[r, :].astype(F32)
        q_f = (qc * e_fwd).astype(BF16)
        k_f = (kc * e_rev).astype(BF16)
        q_r = (qc * e_rev).astype(BF16)
        k_r = (kc * e_fwd).astype(BF16)
        k_l = (kc * jnp.exp(last - bc)).astype(BF16)
        q_i = (qc * jnp.exp(bc)).astype(BF16)
        decay = jnp.exp(last)
        for h in range(heads):
            ck = slice(h * dk, (h + 1) * dk)
            cv = slice(h * dv, (h + 1) * dv)
            vh = v_ref[r, cv]
            att = jnp.where(lower, _dot_nt(q_f[:, ck], k_f[:, ck]), _dot_nt(q_r[:, ck], k_r[:, ck]))
            st = states[h]
            o = _dot(att.astype(BF16), vh) + _dot_nt(q_i[:, ck], st.astype(BF16))
            states[h] = st * decay[:, ck] + _dot_tn(vh, k_l[:, ck])
            o = o * lax.rsqrt(jnp.mean(o * o, axis=-1, keepdims=True) + EPS) * gain
            og = og_ref[r, cv].astype(F32)
            o_ref[r, cv] = (o * (og * jax.nn.sigmoid(og))).astype(o_ref.dtype)
    for h in range(heads):
        state_ref[h] = states[h]


def _gla(proj, a_lr, w_up, b_up, head_gain, *, batch, seq, heads, dk, dv, cols, ts):
    m = proj.shape[0]
    ns = seq // ts
    dkt, dvt = heads * dk, heads * dv
    rank_pad = a_lr.shape[1]

    def rows(b, s):
        return b * ns + s

    kernel = functools.partial(_gla_kernel, heads=heads, dk=dk, dv=dv, ts=ts)
    vmem = 2 * ts * (2 * dkt + 2 * dvt + rank_pad) * 2 + 2 * ts * dvt * 2 + heads * dk * dv * 4
    return pl.pallas_call(
        kernel,
        out_shape=jax.ShapeDtypeStruct((m, dvt), BF16),
        grid=(batch, ns),
        in_specs=[
            pl.BlockSpec((ts, dkt), lambda b, s: (rows(b, s), cols["q"] // dkt)),
            pl.BlockSpec((ts, dkt), lambda b, s: (rows(b, s), cols["k"] // dkt)),
            pl.BlockSpec((ts, dvt), lambda b, s: (rows(b, s), cols["v"] // dvt)),
            pl.BlockSpec((ts, dvt), lambda b, s: (rows(b, s), cols["og"] // dvt)),
            pl.BlockSpec((ts, rank_pad), lambda b, s: (rows(b, s), 0)),
            _resident(w_up.shape),
            _resident(b_up.shape),
            _resident(head_gain.shape),
        ],
        out_specs=pl.BlockSpec((ts, dvt), lambda b, s: (rows(b, s), 0)),
        scratch_shapes=[pltpu.VMEM((heads, dv, dk), F32)],
        compiler_params=_params(("parallel", "arbitrary"), vmem + (24 << 20)),
        name="gla",
    )(proj, proj, proj, proj, a_lr, w_up, b_up, head_gain)


def _mix_out_kernel(ax_ref, ab_ref, ac_ref, axp_ref, acp_ref, o_ref, ga_ref, gb_ref, x_ref,
                    cw_ref, wa_ref, wb_ref, wm_ref, g_ref, out_ref, *, tiles_per_seq):
    u = ac_ref[...].astype(F32) * ax_ref[...].astype(F32)
    u_prev = acp_ref[...].astype(F32) * axp_ref[...].astype(F32)
    first = (pl.program_id(0) % tiles_per_seq) == 0
    u_prev = jnp.where(first, 0.0, u_prev)
    p1 = u_prev[SUBLANES - 1:SUBLANES, :]
    p2 = u_prev[SUBLANES - 2:SUBLANES - 1, :]
    row = lax.broadcasted_iota(jnp.int32, u.shape, 0)
    u1 = jnp.where(row == 0, p1, pltpu.roll(u, 1, 0))
    u2 = jnp.where(row == 0, p2, jnp.where(row == 1, p1, pltpu.roll(u, 2, 0)))
    cw = cw_ref[...]
    y = cw[0:1, :] * u2 + cw[1:2, :] * u1 + cw[2:3, :] * u
    ca = (ab_ref[...].astype(F32) * y).astype(BF16)

    ya = _dot(ca, wa_ref[...])
    yb = _dot(o_ref[...], wb_ref[...])
    mix = jax.nn.sigmoid(ga_ref[...].astype(F32)) * ya + jax.nn.sigmoid(gb_ref[...].astype(F32)) * yb
    m = _dot(mix.astype(BF16), wm_ref[...])
    out_ref[...] = x_ref[...] + _rms(m, g_ref[...])


def _mix_out(proj, o, x2, conv_w, wa, wb, wm, gain, *, seq, cols, bm):
    m, d = x2.shape
    ca = conv_w.shape[1]
    cb = o.shape[1]
    tiles_per_seq = seq // bm
    halo = bm // SUBLANES

    def prev(i):
        return jnp.maximum(i * halo - 1, 0)

    kernel = functools.partial(_mix_out_kernel, tiles_per_seq=tiles_per_seq)
    vmem = (2 * bm * (3 * ca + cb + 2 * d) * 2 + 4 * bm * d * 4
            + (ca + cb + d) * d * 2 + 4 * bm * d * 4)
    return pl.pallas_call(
        kernel,
        out_shape=jax.ShapeDtypeStruct((m, d), F32),
        grid=(m // bm,),
        in_specs=[
            pl.BlockSpec((bm, ca), lambda i: (i, cols["a_x"] // ca)),
            pl.BlockSpec((bm, ca), lambda i: (i, cols["a_b"] // ca)),
            pl.BlockSpec((bm, ca), lambda i: (i, cols["a_c"] // ca)),
            pl.BlockSpec((SUBLANES, ca), lambda i: (prev(i), cols["a_x"] // ca)),
            pl.BlockSpec((SUBLANES, ca), lambda i: (prev(i), cols["a_c"] // ca)),
            pl.BlockSpec((bm, cb), lambda i: (i, 0)),
            pl.BlockSpec((bm, d), lambda i: (i, cols["gate_a"] // d)),
            pl.BlockSpec((bm, d), lambda i: (i, cols["gate_b"] // d)),
            pl.BlockSpec((bm, d), lambda i: (i, 0)),
            _resident(conv_w.shape),
            _resident(wa.shape),
            _resident(wb.shape),
            _resident(wm.shape),
            _resident(gain.shape),
        ],
        out_specs=pl.BlockSpec((bm, d), lambda i: (i, 0)),
        compiler_params=_params(("arbitrary",), vmem + (8 << 20)),
        name="mix_out",
    )(proj, proj, proj, proj, proj, o, proj, proj, x2, conv_w, wa, wb, wm, gain)


def _ffn_kernel(x_ref, gpre_ref, wg_ref, wu_ref, wd_ref, gpost_ref, out_ref, h_ref, acc_ref):
    j = pl.program_id(1)

    @pl.when(j == 0)
    def _():
        h_ref[...] = _rms(x_ref[...], gpre_ref[...]).astype(BF16)
        acc_ref[...] = jnp.zeros_like(acc_ref)

    h = h_ref[...]
    g = _dot(h, wg_ref[...])
    u = _dot(h, wu_ref[...])
    a = (g * jax.nn.sigmoid(g) * u).astype(BF16)
    acc_ref[...] += _dot(a, wd_ref[...])

    @pl.when(j == pl.num_programs(1) - 1)
    def _():
        out_ref[...] = x_ref[...] + _rms(acc_ref[...], gpost_ref[...])


def _ffn(x2, g_pre, wg, wu, wd, g_post, *, bm, bf):
    m, d = x2.shape
    f = wg.shape[1]
    vmem = 4 * bm * d * 4 + 2 * 3 * d * bf * 2 + bm * d * 2 + bm * d * 4 + 3 * bm * bf * 4
    return pl.pallas_call(
        _ffn_kernel,
        out_shape=jax.ShapeDtypeStruct((m, d), F32),
        grid=(m // bm, f // bf),
        in_specs=[
            pl.BlockSpec((bm, d), lambda i, j: (i, 0)),
            pl.BlockSpec((1, d), lambda i, j: (0, 0)),
            pl.BlockSpec((d, bf), lambda i, j: (0, j)),
            pl.BlockSpec((d, bf), lambda i, j: (0, j)),
            pl.BlockSpec((bf, d), lambda i, j: (j, 0)),
            pl.BlockSpec((1, d), lambda i, j: (0, 0)),
        ],
        out_specs=pl.BlockSpec((bm, d), lambda i, j: (i, 0)),
        scratch_shapes=[pltpu.VMEM((bm, d), BF16), pltpu.VMEM((bm, d), F32)],
        compiler_params=_params(("parallel", "arbitrary"), vmem + (8 << 20)),
        name="ffn",
    )(x2, g_pre, wg, wu, wd, g_post)


def _ple_kernel(x_ref, p_ref, gpre_ref, wpg_ref, wpp_ref, gpost_ref, out_ref):
    x = x_ref[...]
    h = _rms(x, gpre_ref[...]).astype(BF16)
    e = jax.nn.sigmoid(_dot(h, wpg_ref[...])) * _dot(p_ref[...].astype(BF16), wpp_ref[...])
    out_ref[...] = x + _rms(e, gpost_ref[...])


def _ple(x2, p2, g_pre, wpg, wpp, g_post, *, bm):
    m, d = x2.shape
    pd = p2.shape[1]
    vmem = 4 * bm * d * 4 + 2 * bm * pd * 4 + (d + pd) * d * 2 + 4 * bm * d * 4
    return pl.pallas_call(
        _ple_kernel,
        out_shape=jax.ShapeDtypeStruct((m, d), F32),
        grid=(m // bm,),
        in_specs=[
            pl.BlockSpec((bm, d), lambda i: (i, 0)),
            pl.BlockSpec((bm, pd), lambda i: (i, 0)),
            _resident(g_pre.shape),
            _resident(wpg.shape),
            _resident(wpp.shape),
            _resident(g_post.shape),
        ],
        out_specs=pl.BlockSpec((bm, d), lambda i: (i, 0)),
        compiler_params=_params(("parallel",), vmem + (8 << 20)),
        name="ple",
    )(x2, p2, g_pre, wpg, wpp, g_post)


def _proj_layout(mix_a, dk_total, mix_b, d):
    names = ["a_x", "a_b", "a_c", "q", "k", "v", "og", "gate_a", "gate_b"]
    widths = [mix_a, mix_a, mix_a, dk_total, dk_total, mix_b, mix_b, d, d]
    cols, off = {}, 0
    for nme, wdt in zip(names, widths):
        assert off % wdt == 0, (nme, off, wdt)
        cols[nme] = off
        off += wdt
    return cols, cols["gate_a"]


def kernel(x, p, w_in, conv_w, w_a_out, w_alpha_up, b_alpha_up, gla_head_gain, w_b_out, w_mix_out, g_pre_mix, g_post_mix, g_pre_ffn, g_post_ffn, w_ff_gate, w_ff_up, w_ff_down, g_pre_ple, g_post_ple, w_ple_gate, w_ple_proj):
    batch, seq, d = x.shape
    depth = w_in.shape[0]
    mix_a = conv_w.shape[-1]
    rank, dk_total = w_alpha_up.shape[1], w_alpha_up.shape[2]
    dv = gla_head_gain.shape[-1]
    mix_b = w_b_out.shape[1]
    heads = mix_b // dv
    dk = dk_total // heads
    assert rank <= LANES and seq % CHUNK == 0
    cols, lr0 = _proj_layout(mix_a, dk_total, mix_b, d)

    xs = x.reshape(batch * seq, d)
    for i in range(depth):
        proj, a_lr = _norm_proj(xs, g_pre_mix[i][None], w_in[i].T, lr0=lr0, rank=rank,
                                bm=1024, bn=1024)
        o = _gla(proj, a_lr, w_alpha_up[i], b_alpha_up[i][None], gla_head_gain[i][None],
                 batch=batch, seq=seq, heads=heads, dk=dk, dv=dv, cols=cols, ts=256)
        xs = _mix_out(proj, o, xs, conv_w[i], w_a_out[i].astype(BF16), w_b_out[i].astype(BF16),
                      w_mix_out[i].astype(BF16), g_post_mix[i][None], seq=seq, cols=cols, bm=256)
        xs = _ffn(xs, g_pre_ffn[i][None], w_ff_gate[i].astype(BF16), w_ff_up[i].astype(BF16),
                  w_ff_down[i].astype(BF16), g_post_ffn[i][None], bm=512, bf=512)
        xs = _ple(xs, p[i].reshape(batch * seq, -1), g_pre_ple[i][None], w_ple_gate[i].astype(BF16),
                  w_ple_proj[i].astype(BF16), g_post_ple[i][None], bm=512)
    return xs.reshape(batch, seq, d)
```

```python
import functools

import jax
import jax.numpy as jnp
from jax import lax
from jax.experimental import pallas as pl
from jax.experimental.pallas import tpu as pltpu

EPS = 1e-6
CHUNK = 64
GLA_TAU = 16.0
CONV_WIDTH = 3
LANES = 128
SUBLANES = 8
VMEM_LIMIT_CAP = 60000 * 1024

F32 = jnp.float32
BF16 = jnp.bfloat16


def _rms(x, gain):
    ms = jnp.mean(x * x, axis=-1, keepdims=True)
    return x * lax.rsqrt(ms + EPS) * gain


def _dot(a, b):
    return jnp.dot(a, b, preferred_element_type=F32)


def _dot_nt(a, b):
    return lax.dot_general(a, b, (((1,), (1,)), ((), ())), preferred_element_type=F32)


def _dot_tn(a, b):
    return lax.dot_general(a, b, (((0,), (0,)), ((), ())), preferred_element_type=F32)


def _params(semantics, vmem_bytes):
    return pltpu.CompilerParams(
        dimension_semantics=semantics,
        vmem_limit_bytes=min(int(vmem_bytes), VMEM_LIMIT_CAP),
    )


def _resident(shape):
    return pl.BlockSpec(shape, lambda *_: (0,) * len(shape), pipeline_mode=pl.Buffered(1))


def _norm_proj_kernel(x_ref, g_ref, wt_ref, wlr_ref, o_ref, lr_ref, h_ref):
    @pl.when(pl.program_id(1) == 0)
    def _():
        h = _rms(x_ref[...], g_ref[...]).astype(BF16)
        h_ref[...] = h
        wlr = wlr_ref[...]
        wlr = jnp.concatenate([wlr, jnp.zeros((LANES - wlr.shape[0], wlr.shape[1]), F32)], axis=0)
        lr_ref[...] = _dot_nt(h, wlr.astype(BF16)).astype(lr_ref.dtype)

    o_ref[...] = _dot_nt(h_ref[...], wt_ref[...].astype(BF16)).astype(o_ref.dtype)


def _norm_proj(x2, gain, wt, *, lr0, rank, bm, bn):
    m, d = x2.shape
    n = wt.shape[0] - rank
    assert lr0 % bn == 0 and n % bn == 0 and rank % SUBLANES == 0

    def w_rows(i, j):
        skip = jnp.where(j * bn >= lr0, rank // SUBLANES, 0)
        return ((j * (bn // SUBLANES) + skip) * SUBLANES, 0)

    vmem = 2 * bm * d * 4 + 2 * bn * d * 4 + 2 * bm * bn * 2 + bm * d * 2 + bm * bn * 4 + d * bn * 2
    return pl.pallas_call(
        _norm_proj_kernel,
        out_shape=(jax.ShapeDtypeStruct((m, n), BF16), jax.ShapeDtypeStruct((m, LANES), BF16)),
        grid=(m // bm, n // bn),
        in_specs=[
            pl.BlockSpec((bm, d), lambda i, j: (i, 0)),
            pl.BlockSpec((1, d), lambda i, j: (0, 0)),
            pl.BlockSpec((pl.Element(bn), pl.Element(d)), w_rows),
            pl.BlockSpec((pl.Element(rank), pl.Element(d)), lambda i, j: (lr0, 0)),
        ],
        out_specs=(pl.BlockSpec((bm, bn), lambda i, j: (i, j)),
                   pl.BlockSpec((bm, LANES), lambda i, j: (i, 0))),
        scratch_shapes=[pltpu.VMEM((bm, d), BF16)],
        compiler_params=_params(("parallel", "arbitrary"), vmem + (8 << 20)),
        name="norm_proj",
    )(x2, gain, wt, wt)


def _gla_kernel(q_ref, k_ref, v_ref, og_ref, alr_ref, wup_ref, bup_ref, hg_ref,
                o_ref, state_ref, *, heads, dk, dv, ts):
    @pl.when(pl.program_id(1) == 0)
    def _():
        state_ref[...] = jnp.zeros_like(state_ref)

    w_up = wup_ref[...]
    rank = w_up.shape[0]
    w_up = jnp.concatenate([w_up, jnp.zeros((LANES - rank, w_up.shape[1]), F32)], axis=0)
    z = _dot(alr_ref[...], w_up.astype(BF16)) + bup_ref[...]
    log_a = (jnp.minimum(z, 0.0) - jnp.log1p(jnp.exp(-jnp.abs(z)))) * (1.0 / GLA_TAU)

    row = lax.broadcasted_iota(jnp.int32, (ts, ts), 0)
    col = lax.broadcasted_iota(jnp.int32, (ts, ts), 1)
    tri = ((row // CHUNK == col // CHUNK) & (col <= row)).astype(BF16)
    la_hi = log_a.astype(BF16)
    la_lo = (log_a - la_hi.astype(F32)).astype(BF16)
    b = _dot(tri, la_hi) + _dot(tri, la_lo)

    crow = lax.broadcasted_iota(jnp.int32, (CHUNK, CHUNK), 0)
    ccol = lax.broadcasted_iota(jnp.int32, (CHUNK, CHUNK), 1)
    lower = ccol <= crow

    scale = dk ** -0.5
    gain = hg_ref[...]
    states = [state_ref[h] for h in range(heads)]
    for c in range(ts // CHUNK):
        r = slice(c * CHUNK, (c + 1) * CHUNK)
        bc = b[r, :]
        mid = bc[CHUNK // 2:CHUNK // 2 + 1, :]
        last = bc[CHUNK - 1:CHUNK, :]
        e_fwd = jnp.exp(bc - mid)
        e_rev = jnp.exp(mid - bc)
        qc = q_ref[r, :].astype(F32) * scale
        kc = k_ref[r, :].astype(F32)
        q_f = (qc * e_fwd).astype(BF16)
        k_f = (kc * e_rev).astype(BF16)
        q_r = (qc * e_rev).astype(BF16)
        k_r = (kc * e_fwd).astype(BF16)
        k_l = (kc * jnp.exp(last - bc)).astype(BF16)
        q_i = (qc * jnp.exp(bc)).astype(BF16)
        decay = jnp.exp(last)
        for h in range(heads):
            ck = slice(h * dk, (h + 1) * dk)
            cv = slice(h * dv, (h + 1) * dv)
            vh = v_ref[r, cv]
            att = jnp.where(lower, _dot_nt(q_f[:, ck], k_f[:, ck]), _dot_nt(q_r[:, ck], k_r[:, ck]))
            st = states[h]
            o = _dot(att.astype(BF16), vh) + _dot_nt(q_i[:, ck], st.astype(BF16))
            states[h] = st * decay[:, ck] + _dot_tn(vh, k_l[:, ck])
            o = o * lax.rsqrt(jnp.mean(o * o, axis=-1, keepdims=True) + EPS) * gain
            og = og_ref[r, cv].astype(F32)
            o_ref[r, cv] = (o * (og * jax.nn.sigmoid(og))).astype(o_ref.dtype)
    for h in range(heads):
        state_ref[h] = states[h]


def _gla(proj, a_lr, w_up, b_up, head_gain, *, batch, seq, heads, dk, dv, cols, ts):
    m = proj.shape[0]
    ns = seq // ts
    dkt, dvt = heads * dk, heads * dv
    rank_pad = a_lr.shape[1]

    def rows(b, s):
        return b * ns + s

    kernel = functools.partial(_gla_kernel, heads=heads, dk=dk, dv=dv, ts=ts)
    vmem = 2 * ts * (2 * dkt + 2 * dvt + rank_pad) * 2 + 2 * ts * dvt * 2 + heads * dk * dv * 4
    return pl.pallas_call(
        kernel,
        out_shape=jax.ShapeDtypeStruct((m, dvt), BF16),
        grid=(batch, ns),
        in_specs=[
            pl.BlockSpec((ts, dkt), lambda b, s: (rows(b, s), cols["q"] // dkt)),
            pl.BlockSpec((ts, dkt), lambda b, s: (rows(b, s), cols["k"] // dkt)),
            pl.BlockSpec((ts, dvt), lambda b, s: (rows(b, s), cols["v"] // dvt)),
            pl.BlockSpec((ts, dvt), lambda b, s: (rows(b, s), cols["og"] // dvt)),
            pl.BlockSpec((ts, rank_pad), lambda b, s: (rows(b, s), 0)),
            _resident(w_up.shape),
            _resident(b_up.shape),
            _resident(head_gain.shape),
        ],
        out_specs=pl.BlockSpec((ts, dvt), lambda b, s: (rows(b, s), 0)),
        scratch_shapes=[pltpu.VMEM((heads, dv, dk), F32)],
        compiler_params=_params(("parallel", "arbitrary"), vmem + (24 << 20)),
        name="gla",
    )(proj, proj, proj, proj, a_lr, w_up, b_up, head_gain)


def _mix_out_kernel(ax_ref, ab_ref, ac_ref, axp_ref, acp_ref, o_ref, ga_ref, gb_ref, x_ref,
                    cw_ref, wa_ref, wb_ref, wm_ref, g_ref, out_ref, *, tiles_per_seq):
    u = ac_ref[...].astype(F32) * ax_ref[...].astype(F32)
    u_prev = acp_ref[...].astype(F32) * axp_ref[...].astype(F32)
    first = (pl.program_id(0) % tiles_per_seq) == 0
    u_prev = jnp.where(first, 0.0, u_prev)
    p1 = u_prev[SUBLANES - 1:SUBLANES, :]
    p2 = u_prev[SUBLANES - 2:SUBLANES - 1, :]
    row = lax.broadcasted_iota(jnp.int32, u.shape, 0)
    u1 = jnp.where(row == 0, p1, pltpu.roll(u, 1, 0))
    u2 = jnp.where(row == 0, p2, jnp.where(row == 1, p1, pltpu.roll(u, 2, 0)))
    cw = cw_ref[...]
    y = cw[0:1, :] * u2 + cw[1:2, :] * u1 + cw[2:3, :] * u
    ca = (ab_ref[...].astype(F32) * y).astype(BF16)

    ya = _dot(ca, wa_ref[...])
    yb = _dot(o_ref[...], wb_ref[...])
    mix = jax.nn.sigmoid(ga_ref[...].astype(F32)) * ya + jax.nn.sigmoid(gb_ref[...].astype(F32)) * yb
    m = _dot(mix.astype(BF16), wm_ref[...])
    out_ref[...] = x_ref[...] + _rms(m, g_ref[...])


def _mix_out(proj, o, x2, conv_w, wa, wb, wm, gain, *, seq, cols, bm):
    m, d = x2.shape
    ca = conv_w.shape[1]
    cb = o.shape[1]
    tiles_per_seq = seq // bm
    halo = bm // SUBLANES

    def prev(i):
        return jnp.maximum(i * halo - 1, 0)

    kernel = functools.partial(_mix_out_kernel, tiles_per_seq=tiles_per_seq)
    vmem = (2 * bm * (3 * ca + cb + 2 * d) * 2 + 4 * bm * d * 4
            + (ca + cb + d) * d * 2 + 4 * bm * d * 4)
    return pl.pallas_call(
        kernel,
        out_shape=jax.ShapeDtypeStruct((m, d), F32),
        grid=(m // bm,),
        in_specs=[
            pl.BlockSpec((bm, ca), lambda i: (i, cols["a_x"] // ca)),
            pl.BlockSpec((bm, ca), lambda i: (i, cols["a_b"] // ca)),
            pl.BlockSpec((bm, ca), lambda i: (i, cols["a_c"] // ca)),
            pl.BlockSpec((SUBLANES, ca), lambda i: (prev(i), cols["a_x"] // ca)),
            pl.BlockSpec((SUBLANES, ca), lambda i: (prev(i), cols["a_c"] // ca)),
            pl.BlockSpec((bm, cb), lambda i: (i, 0)),
            pl.BlockSpec((bm, d), lambda i: (i, cols["gate_a"] // d)),
            pl.BlockSpec((bm, d), lambda i: (i, cols["gate_b"] // d)),
            pl.BlockSpec((bm, d), lambda i: (i, 0)),
            _resident(conv_w.shape),
            _resident(wa.shape),
            _resident(wb.shape),
            _resident(wm.shape),
            _resident(gain.shape),
        ],
        out_specs=pl.BlockSpec((bm, d), lambda i: (i, 0)),
        compiler_params=_params(("arbitrary",), vmem + (8 << 20)),
        name="mix_out",
    )(proj, proj, proj, proj, proj, o, proj, proj, x2, conv_w, wa, wb, wm, gain)


def _ffn_kernel(x_ref, gpre_ref, wg_ref, wu_ref, wd_ref, gpost_ref, out_ref, h_ref):
    j = pl.program_id(1)

    @pl.when(j == 0)
    def _():
        h_ref[...] = _rms(x_ref[...], gpre_ref[...]).astype(BF16)
        out_ref[...] = jnp.zeros_like(out_ref)

    h = h_ref[...]
    g = _dot(h, wg_ref[...].astype(BF16))
    u = _dot(h, wu_ref[...].astype(BF16))
    a = (g * jax.nn.sigmoid(g) * u).astype(BF16)
    out_ref[...] += _dot(a, wd_ref[...].astype(BF16))

    @pl.when(j == pl.num_programs(1) - 1)
    def _():
        out_ref[...] = x_ref[...] + _rms(out_ref[...], gpost_ref[...])


def _ffn(x2, g_pre, wg, wu, wd, g_post, *, bm, bf):
    m, d = x2.shape
    f = wg.shape[1]
    vmem = (bm * d * 4 + 2 * bm * d * 4 + 2 * 3 * d * bf * 4 + bm * d * 2
            + 3 * bm * bf * 4 + 3 * d * bf * 2)
    return pl.pallas_call(
        _ffn_kernel,
        out_shape=jax.ShapeDtypeStruct((m, d), F32),
        grid=(m // bm, f // bf),
        in_specs=[
            pl.BlockSpec((bm, d), lambda i, j: (i, 0), pipeline_mode=pl.Buffered(1)),
            pl.BlockSpec((1, d), lambda i, j: (0, 0)),
            pl.BlockSpec((d, bf), lambda i, j: (0, j)),
            pl.BlockSpec((d, bf), lambda i, j: (0, j)),
            pl.BlockSpec((bf, d), lambda i, j: (j, 0)),
            pl.BlockSpec((1, d), lambda i, j: (0, 0)),
        ],
        out_specs=pl.BlockSpec((bm, d), lambda i, j: (i, 0)),
        scratch_shapes=[pltpu.VMEM((bm, d), BF16)],
        compiler_params=_params(("parallel", "arbitrary"), vmem + (8 << 20)),
        name="ffn",
    )(x2, g_pre, wg, wu, wd, g_post)


def _ple_kernel(x_ref, p_ref, gpre_ref, wpg_ref, wpp_ref, gpost_ref, out_ref):
    x = x_ref[...]
    h = _rms(x, gpre_ref[...]).astype(BF16)
    e = jax.nn.sigmoid(_dot(h, wpg_ref[...])) * _dot(p_ref[...].astype(BF16), wpp_ref[...])
    out_ref[...] = x + _rms(e, gpost_ref[...])


def _ple(x2, p2, g_pre, wpg, wpp, g_post, *, bm):
    m, d = x2.shape
    pd = p2.shape[1]
    vmem = 4 * bm * d * 4 + 2 * bm * pd * 4 + (d + pd) * d * 2 + 4 * bm * d * 4
    return pl.pallas_call(
        _ple_kernel,
        out_shape=jax.ShapeDtypeStruct((m, d), F32),
        grid=(m // bm,),
        in_specs=[
            pl.BlockSpec((bm, d), lambda i: (i, 0)),
            pl.BlockSpec((bm, pd), lambda i: (i, 0)),
            _resident(g_pre.shape),
            _resident(wpg.shape),
            _resident(wpp.shape),
            _resident(g_post.shape),
        ],
        out_specs=pl.BlockSpec((bm, d), lambda i: (i, 0)),
        compiler_params=_params(("parallel",), vmem + (8 << 20)),
        name="ple",
    )(x2, p2, g_pre, wpg, wpp, g_post)


def _proj_layout(mix_a, dk_total, mix_b, d):
    names = ["a_x", "a_b", "a_c", "q", "k", "v", "og", "gate_a", "gate_b"]
    widths = [mix_a, mix_a, mix_a, dk_total, dk_total, mix_b, mix_b, d, d]
    cols, off = {}, 0
    for nme, wdt in zip(names, widths):
        assert off % wdt == 0, (nme, off, wdt)
        cols[nme] = off
        off += wdt
    return cols, cols["gate_a"]


def kernel(x, p, w_in, conv_w, w_a_out, w_alpha_up, b_alpha_up, gla_head_gain, w_b_out, w_mix_out, g_pre_mix, g_post_mix, g_pre_ffn, g_post_ffn, w_ff_gate, w_ff_up, w_ff_down, g_pre_ple, g_post_ple, w_ple_gate, w_ple_proj):
    batch, seq, d = x.shape
    depth = w_in.shape[0]
    mix_a = conv_w.shape[-1]
    rank, dk_total = w_alpha_up.shape[1], w_alpha_up.shape[2]
    dv = gla_head_gain.shape[-1]
    mix_b = w_b_out.shape[1]
    heads = mix_b // dv
    dk = dk_total // heads
    assert rank <= LANES and seq % CHUNK == 0
    cols, lr0 = _proj_layout(mix_a, dk_total, mix_b, d)

    xs = x.reshape(batch * seq, d)
    for i in range(depth):
        proj, a_lr = _norm_proj(xs, g_pre_mix[i][None], w_in[i].T, lr0=lr0, rank=rank,
                                bm=1024, bn=1024)
        o = _gla(proj, a_lr, w_alpha_up[i], b_alpha_up[i][None], gla_head_gain[i][None],
                 batch=batch, seq=seq, heads=heads, dk=dk, dv=dv, cols=cols, ts=256)
        xs = _mix_out(proj, o, xs, conv_w[i], w_a_out[i].astype(BF16), w_b_out[i].astype(BF16),
                      w_mix_out[i].astype(BF16), g_post_mix[i][None], seq=seq, cols=cols, bm=256)
        xs = _ffn(xs, g_pre_ffn[i][None], w_ff_gate[i], w_ff_up[i], w_ff_down[i],
                  g_post_ffn[i][None], bm=1024, bf=256)
        xs = _ple(xs, p[i].reshape(batch * seq, -1), g_pre_ple[i][None], w_ple_gate[i].astype(BF16),
                  w_ple_proj[i].astype(BF16), g_post_ple[i][None], bm=512)
    return xs.reshape(batch, seq, d)
```

```python
import functools

import jax
import jax.numpy as jnp
from jax import lax
from jax.experimental import pallas as pl
from jax.experimental.pallas import tpu as pltpu

EPS = 1e-6
CHUNK = 64
GLA_TAU = 16.0
CONV_WIDTH = 3
LANES = 128
SUBLANES = 8
VMEM_LIMIT_CAP = 60000 * 1024

F32 = jnp.float32
BF16 = jnp.bfloat16


def _rms(x, gain):
    ms = jnp.mean(x * x, axis=-1, keepdims=True)
    return x * lax.rsqrt(ms + EPS) * gain


def _dot(a, b):
    return jnp.dot(a, b, preferred_element_type=F32)


def _dot_nt(a, b):
    return lax.dot_general(a, b, (((1,), (1,)), ((), ())), preferred_element_type=F32)


def _dot_tn(a, b):
    return lax.dot_general(a, b, (((0,), (0,)), ((), ())), preferred_element_type=F32)


def _params(semantics, vmem_bytes):
    return pltpu.CompilerParams(
        dimension_semantics=semantics,
        vmem_limit_bytes=min(int(vmem_bytes), VMEM_LIMIT_CAP),
    )


def _resident(shape):
    return pl.BlockSpec(shape, lambda *_: (0,) * len(shape), pipeline_mode=pl.Buffered(1))


BF16_ROWS = 16


def _cast_plan(weights, n_steps, step_of):
    in_specs, out_specs, out_shapes, vmem = [], [], [], 0
    for w in weights:
        rows, width = w.shape
        assert rows % (n_steps * BF16_ROWS) == 0, (w.shape, n_steps)
        slab = (rows // n_steps, width)
        spec = pl.BlockSpec(slab, lambda *idx: (step_of(*idx), 0))
        in_specs.append(spec)
        out_specs.append(spec)
        out_shapes.append(jax.ShapeDtypeStruct(w.shape, BF16))
        vmem += 2 * slab[0] * slab[1] * (4 + 2)
    return in_specs, out_specs, out_shapes, vmem


def _cast_slabs(src_refs, dst_refs):
    for src, dst in zip(src_refs, dst_refs):
        dst[...] = src[...].astype(dst.dtype)


def _norm_proj_kernel(x_ref, g_ref, wt_ref, wlr_ref, o_ref, lr_ref, h_ref):
    @pl.when(pl.program_id(1) == 0)
    def _():
        h = _rms(x_ref[...], g_ref[...]).astype(BF16)
        h_ref[...] = h
        wlr = wlr_ref[...]
        wlr = jnp.concatenate([wlr, jnp.zeros((LANES - wlr.shape[0], wlr.shape[1]), F32)], axis=0)
        lr_ref[...] = _dot_nt(h, wlr.astype(BF16)).astype(lr_ref.dtype)

    o_ref[...] = _dot_nt(h_ref[...], wt_ref[...].astype(BF16)).astype(o_ref.dtype)


def _norm_proj(x2, gain, wt, *, lr0, rank, bm, bn):
    m, d = x2.shape
    n = wt.shape[0] - rank
    assert lr0 % bn == 0 and n % bn == 0 and rank % SUBLANES == 0

    def w_rows(i, j):
        skip = jnp.where(j * bn >= lr0, rank // SUBLANES, 0)
        return ((j * (bn // SUBLANES) + skip) * SUBLANES, 0)

    vmem = 2 * bm * d * 4 + 2 * bn * d * 4 + 2 * bm * bn * 2 + bm * d * 2 + bm * bn * 4 + d * bn * 2
    return pl.pallas_call(
        _norm_proj_kernel,
        out_shape=(jax.ShapeDtypeStruct((m, n), BF16), jax.ShapeDtypeStruct((m, LANES), BF16)),
        grid=(m // bm, n // bn),
        in_specs=[
            pl.BlockSpec((bm, d), lambda i, j: (i, 0)),
            pl.BlockSpec((1, d), lambda i, j: (0, 0)),
            pl.BlockSpec((pl.Element(bn), pl.Element(d)), w_rows),
            pl.BlockSpec((pl.Element(rank), pl.Element(d)), lambda i, j: (lr0, 0)),
        ],
        out_specs=(pl.BlockSpec((bm, bn), lambda i, j: (i, j)),
                   pl.BlockSpec((bm, LANES), lambda i, j: (i, 0))),
        scratch_shapes=[pltpu.VMEM((bm, d), BF16)],
        compiler_params=_params(("parallel", "arbitrary"), vmem + (8 << 20)),
        name="norm_proj",
    )(x2, gain, wt, wt)


def _gla_kernel(*refs, heads, dk, dv, ts, n_cast):
    q_ref, k_ref, v_ref, og_ref, alr_ref, wup_ref, bup_ref, hg_ref = refs[:8]
    o_ref = refs[8 + n_cast]
    state_ref = refs[-1]
    _cast_slabs(refs[8:8 + n_cast], refs[9 + n_cast:9 + 2 * n_cast])

    @pl.when(pl.program_id(1) == 0)
    def _():
        state_ref[...] = jnp.zeros_like(state_ref)

    w_up = wup_ref[...]
    rank = w_up.shape[0]
    w_up = jnp.concatenate([w_up, jnp.zeros((LANES - rank, w_up.shape[1]), F32)], axis=0)
    z = _dot(alr_ref[...], w_up.astype(BF16)) + bup_ref[...]
    log_a = (jnp.minimum(z, 0.0) - jnp.log1p(jnp.exp(-jnp.abs(z)))) * (1.0 / GLA_TAU)

    row = lax.broadcasted_iota(jnp.int32, (ts, ts), 0)
    col = lax.broadcasted_iota(jnp.int32, (ts, ts), 1)
    tri = ((row // CHUNK == col // CHUNK) & (col <= row)).astype(BF16)
    la_hi = log_a.astype(BF16)
    la_lo = (log_a - la_hi.astype(F32)).astype(BF16)
    b = _dot(tri, la_hi) + _dot(tri, la_lo)

    crow = lax.broadcasted_iota(jnp.int32, (CHUNK, CHUNK), 0)
    ccol = lax.broadcasted_iota(jnp.int32, (CHUNK, CHUNK), 1)
    lower = ccol <= crow

    scale = dk ** -0.5
    gain = hg_ref[...]
    states = [state_ref[h] for h in range(heads)]
    for c in range(ts // CHUNK):
        r = slice(c * CHUNK, (c + 1) * CHUNK)
        bc = b[r, :]
        mid = bc[CHUNK // 2:CHUNK // 2 + 1, :]
        last = bc[CHUNK - 1:CHUNK, :]
        e_fwd = jnp.exp(bc - mid)
        e_rev = jnp.exp(mid - bc)
        qc = q_ref[r, :].astype(F32) * scale
        kc = k_ref[r, :].astype(F32)
        q_f = (qc * e_fwd).astype(BF16)
        k_f = (kc * e_rev).astype(BF16)
        q_r = (qc * e_rev).astype(BF16)
        k_r = (kc * e_fwd).astype(BF16)
        k_l = (kc * jnp.exp(last - bc)).astype(BF16)
        q_i = (qc * jnp.exp(bc)).astype(BF16)
        decay = jnp.exp(last)
        for h in range(heads):
            ck = slice(h * dk, (h + 1) * dk)
            cv = slice(h * dv, (h + 1) * dv)
            vh = v_ref[r, cv]
            att = jnp.where(lower, _dot_nt(q_f[:, ck], k_f[:, ck]), _dot_nt(q_r[:, ck], k_r[:, ck]))
            st = states[h]
            o = _dot(att.astype(BF16), vh) + _dot_nt(q_i[:, ck], st.astype(BF16))
            states[h] = st * decay[:, ck] + _dot_tn(vh, k_l[:, ck])
            o = o * lax.rsqrt(jnp.mean(o * o, axis=-1, keepdims=True) + EPS) * gain
            og = og_ref[r, cv].astype(F32)
            o_ref[r, cv] = (o * (og * jax.nn.sigmoid(og))).astype(o_ref.dtype)
    for h in range(heads):
        state_ref[h] = states[h]


def _gla(proj, a_lr, w_up, b_up, head_gain, cast_weights, *, batch, seq, heads, dk, dv, cols, ts):
    m = proj.shape[0]
    ns = seq // ts
    dkt, dvt = heads * dk, heads * dv
    rank_pad = a_lr.shape[1]

    def rows(b, s):
        return b * ns + s

    c_in, c_out, c_shapes, c_vmem = _cast_plan(cast_weights, batch * ns, rows)
    kernel = functools.partial(_gla_kernel, heads=heads, dk=dk, dv=dv, ts=ts, n_cast=len(c_in))
    vmem = (2 * ts * (2 * dkt + 2 * dvt + rank_pad) * 2 + 2 * ts * dvt * 2 + heads * dk * dv * 4
            + c_vmem)
    return pl.pallas_call(
        kernel,
        out_shape=[jax.ShapeDtypeStruct((m, dvt), BF16)] + c_shapes,
        grid=(batch, ns),
        in_specs=[
            pl.BlockSpec((ts, dkt), lambda b, s: (rows(b, s), cols["q"] // dkt)),
            pl.BlockSpec((ts, dkt), lambda b, s: (rows(b, s), cols["k"] // dkt)),
            pl.BlockSpec((ts, dvt), lambda b, s: (rows(b, s), cols["v"] // dvt)),
            pl.BlockSpec((ts, dvt), lambda b, s: (rows(b, s), cols["og"] // dvt)),
            pl.BlockSpec((ts, rank_pad), lambda b, s: (rows(b, s), 0)),
            _resident(w_up.shape),
            _resident(b_up.shape),
            _resident(head_gain.shape),
        ] + c_in,
        out_specs=[pl.BlockSpec((ts, dvt), lambda b, s: (rows(b, s), 0))] + c_out,
        scratch_shapes=[pltpu.VMEM((heads, dv, dk), F32)],
        compiler_params=_params(("arbitrary", "arbitrary"), vmem + (24 << 20)),
        name="gla",
    )(proj, proj, proj, proj, a_lr, w_up, b_up, head_gain, *cast_weights)


def _mix_out_kernel(*refs, tiles_per_seq, n_cast):
    (ax_ref, ab_ref, ac_ref, axp_ref, acp_ref, o_ref, ga_ref, gb_ref, x_ref,
     cw_ref, wa_ref, wb_ref, wm_ref, g_ref) = refs[:14]
    out_ref = refs[14 + n_cast]
    _cast_slabs(refs[14:14 + n_cast], refs[15 + n_cast:])

    u = ac_ref[...].astype(F32) * ax_ref[...].astype(F32)
    u_prev = acp_ref[...].astype(F32) * axp_ref[...].astype(F32)
    first = (pl.program_id(0) % tiles_per_seq) == 0
    u_prev = jnp.where(first, 0.0, u_prev)
    p1 = u_prev[SUBLANES - 1:SUBLANES, :]
    p2 = u_prev[SUBLANES - 2:SUBLANES - 1, :]
    row = lax.broadcasted_iota(jnp.int32, u.shape, 0)
    u1 = jnp.where(row == 0, p1, pltpu.roll(u, 1, 0))
    u2 = jnp.where(row == 0, p2, jnp.where(row == 1, p1, pltpu.roll(u, 2, 0)))
    cw = cw_ref[...]
    y = cw[0:1, :] * u2 + cw[1:2, :] * u1 + cw[2:3, :] * u
    ca = (ab_ref[...].astype(F32) * y).astype(BF16)

    ya = _dot(ca, wa_ref[...])
    yb = _dot(o_ref[...], wb_ref[...])
    mix = jax.nn.sigmoid(ga_ref[...].astype(F32)) * ya + jax.nn.sigmoid(gb_ref[...].astype(F32)) * yb
    m = _dot(mix.astype(BF16), wm_ref[...])
    out_ref[...] = x_ref[...] + _rms(m, g_ref[...])


def _mix_out(proj, o, x2, conv_w, wa, wb, wm, gain, cast_weights, *, seq, cols, bm):
    m, d = x2.shape
    ca = conv_w.shape[1]
    cb = o.shape[1]
    tiles_per_seq = seq // bm
    halo = bm // SUBLANES

    def prev(i):
        return jnp.maximum(i * halo - 1, 0)

    c_in, c_out, c_shapes, c_vmem = _cast_plan(cast_weights, m // bm, lambda i: i)
    kernel = functools.partial(_mix_out_kernel, tiles_per_seq=tiles_per_seq, n_cast=len(c_in))
    vmem = (2 * bm * (3 * ca + cb + 2 * d) * 2 + 4 * bm * d * 4
            + (ca + cb + d) * d * 2 + 4 * bm * d * 4 + c_vmem)
    return pl.pallas_call(
        kernel,
        out_shape=[jax.ShapeDtypeStruct((m, d), F32)] + c_shapes,
        grid=(m // bm,),
        in_specs=[
            pl.BlockSpec((bm, ca), lambda i: (i, cols["a_x"] // ca)),
            pl.BlockSpec((bm, ca), lambda i: (i, cols["a_b"] // ca)),
            pl.BlockSpec((bm, ca), lambda i: (i, cols["a_c"] // ca)),
            pl.BlockSpec((SUBLANES, ca), lambda i: (prev(i), cols["a_x"] // ca)),
            pl.BlockSpec((SUBLANES, ca), lambda i: (prev(i), cols["a_c"] // ca)),
            pl.BlockSpec((bm, cb), lambda i: (i, 0)),
            pl.BlockSpec((bm, d), lambda i: (i, cols["gate_a"] // d)),
            pl.BlockSpec((bm, d), lambda i: (i, cols["gate_b"] // d)),
            pl.BlockSpec((bm, d), lambda i: (i, 0)),
            _resident(conv_w.shape),
            _resident(wa.shape),
            _resident(wb.shape),
            _resident(wm.shape),
            _resident(gain.shape),
        ] + c_in,
        out_specs=[pl.BlockSpec((bm, d), lambda i: (i, 0))] + c_out,
        compiler_params=_params(("arbitrary",), vmem + (8 << 20)),
        name="mix_out",
    )(proj, proj, proj, proj, proj, o, proj, proj, x2, conv_w, wa, wb, wm, gain, *cast_weights)


def _ffn_kernel(x_ref, gpre_ref, wg_ref, wu_ref, wd_ref, gpost_ref, out_ref, h_ref):
    j = pl.program_id(1)

    @pl.when(j == 0)
    def _():
        h_ref[...] = _rms(x_ref[...], gpre_ref[...]).astype(BF16)
        out_ref[...] = jnp.zeros_like(out_ref)

    h = h_ref[...]
    g = _dot(h, wg_ref[...])
    u = _dot(h, wu_ref[...])
    a = (g * jax.nn.sigmoid(g) * u).astype(BF16)
    out_ref[...] += _dot(a, wd_ref[...])

    @pl.when(j == pl.num_programs(1) - 1)
    def _():
        out_ref[...] = x_ref[...] + _rms(out_ref[...], gpost_ref[...])


def _ffn(x2, g_pre, wg, wu, wd, g_post, *, bm, bf):
    m, d = x2.shape
    f = wg.shape[1]
    vmem = 3 * bm * d * 4 + 2 * 3 * d * bf * 2 + bm * d * 2 + 3 * bm * bf * 4
    return pl.pallas_call(
        _ffn_kernel,
        out_shape=jax.ShapeDtypeStruct((m, d), F32),
        grid=(m // bm, f // bf),
        in_specs=[
            pl.BlockSpec((bm, d), lambda i, j: (i, 0), pipeline_mode=pl.Buffered(1)),
            pl.BlockSpec((1, d), lambda i, j: (0, 0)),
            pl.BlockSpec((d, bf), lambda i, j: (0, j)),
            pl.BlockSpec((d, bf), lambda i, j: (0, j)),
            pl.BlockSpec((bf, d), lambda i, j: (j, 0)),
            pl.BlockSpec((1, d), lambda i, j: (0, 0)),
        ],
        out_specs=pl.BlockSpec((bm, d), lambda i, j: (i, 0)),
        scratch_shapes=[pltpu.VMEM((bm, d), BF16)],
        compiler_params=_params(("parallel", "arbitrary"), vmem + (8 << 20)),
        name="ffn",
    )(x2, g_pre, wg, wu, wd, g_post)


def _ple_kernel(x_ref, p_ref, gpre_ref, wpg_ref, wpp_ref, gpost_ref, out_ref):
    x = x_ref[...]
    h = _rms(x, gpre_ref[...]).astype(BF16)
    e = jax.nn.sigmoid(_dot(h, wpg_ref[...])) * _dot(p_ref[...].astype(BF16),
                                                      wpp_ref[...].astype(BF16))
    out_ref[...] = x + _rms(e, gpost_ref[...])


def _ple(x2, p2, g_pre, wpg, wpp, g_post, *, bm):
    m, d = x2.shape
    pd = p2.shape[1]
    vmem = 4 * bm * d * 4 + 2 * bm * pd * 4 + d * d * 2 + pd * d * 4 + 4 * bm * d * 4
    return pl.pallas_call(
        _ple_kernel,
        out_shape=jax.ShapeDtypeStruct((m, d), F32),
        grid=(m // bm,),
        in_specs=[
            pl.BlockSpec((bm, d), lambda i: (i, 0)),
            pl.BlockSpec((bm, pd), lambda i: (i, 0)),
            _resident(g_pre.shape),
            _resident(wpg.shape),
            _resident(wpp.shape),
            _resident(g_post.shape),
        ],
        out_specs=pl.BlockSpec((bm, d), lambda i: (i, 0)),
        compiler_params=_params(("parallel",), vmem + (8 << 20)),
        name="ple",
    )(x2, p2, g_pre, wpg, wpp, g_post)


def _proj_layout(mix_a, dk_total, mix_b, d):
    names = ["a_x", "a_b", "a_c", "q", "k", "v", "og", "gate_a", "gate_b"]
    widths = [mix_a, mix_a, mix_a, dk_total, dk_total, mix_b, mix_b, d, d]
    cols, off = {}, 0
    for nme, wdt in zip(names, widths):
        assert off % wdt == 0, (nme, off, wdt)
        cols[nme] = off
        off += wdt
    return cols, cols["gate_a"]


def kernel(x, p, w_in, conv_w, w_a_out, w_alpha_up, b_alpha_up, gla_head_gain, w_b_out, w_mix_out, g_pre_mix, g_post_mix, g_pre_ffn, g_post_ffn, w_ff_gate, w_ff_up, w_ff_down, g_pre_ple, g_post_ple, w_ple_gate, w_ple_proj):
    batch, seq, d = x.shape
    depth = w_in.shape[0]
    mix_a = conv_w.shape[-1]
    rank, dk_total = w_alpha_up.shape[1], w_alpha_up.shape[2]
    dv = gla_head_gain.shape[-1]
    mix_b = w_b_out.shape[1]
    heads = mix_b // dv
    dk = dk_total // heads
    assert rank <= LANES and seq % CHUNK == 0
    cols, lr0 = _proj_layout(mix_a, dk_total, mix_b, d)

    xs = x.reshape(batch * seq, d)
    for i in range(depth):
        proj, a_lr = _norm_proj(xs, g_pre_mix[i][None], w_in[i].T, lr0=lr0, rank=rank,
                                bm=1024, bn=1024)
        o, wa, wb, wm = _gla(
            proj, a_lr, w_alpha_up[i], b_alpha_up[i][None], gla_head_gain[i][None],
            [w_a_out[i], w_b_out[i], w_mix_out[i]],
            batch=batch, seq=seq, heads=heads, dk=dk, dv=dv, cols=cols, ts=256)
        xs, wg, wu, wd, wpg = _mix_out(
            proj, o, xs, conv_w[i], wa, wb, wm, g_post_mix[i][None],
            [w_ff_gate[i], w_ff_up[i], w_ff_down[i], w_ple_gate[i]], seq=seq, cols=cols, bm=256)
        xs = _ffn(xs, g_pre_ffn[i][None], wg, wu, wd, g_post_ffn[i][None], bm=1024, bf=512)
        xs = _ple(xs, p[i].reshape(batch * seq, -1), g_pre_ple[i][None], wpg, w_ple_proj[i],
                  g_post_ple[i][None], bm=512)
    return xs.reshape(batch, seq, d)
```

```python
import functools

import jax
import jax.numpy as jnp
from jax import lax
from jax.experimental import pallas as pl
from jax.experimental.pallas import tpu as pltpu

EPS = 1e-6
CHUNK = 64
GLA_TAU = 16.0
CONV_WIDTH = 3
LANES = 128
SUBLANES = 8
VMEM_LIMIT_CAP = 60000 * 1024

F32 = jnp.float32
BF16 = jnp.bfloat16


def _rms(x, gain):
    ms = jnp.mean(x * x, axis=-1, keepdims=True)
    return x * lax.rsqrt(ms + EPS) * gain


def _dot(a, b):
    return jnp.dot(a, b, preferred_element_type=F32)


def _dot_nt(a, b):
    return lax.dot_general(a, b, (((1,), (1,)), ((), ())), preferred_element_type=F32)


def _dot_tn(a, b):
    return lax.dot_general(a, b, (((0,), (0,)), ((), ())), preferred_element_type=F32)


def _params(semantics, vmem_bytes):
    return pltpu.CompilerParams(
        dimension_semantics=semantics,
        vmem_limit_bytes=min(int(vmem_bytes), VMEM_LIMIT_CAP),
    )


def _resident(shape):
    return pl.BlockSpec(shape, lambda *_: (0,) * len(shape), pipeline_mode=pl.Buffered(1))


BF16_ROWS = 16


def _cast_plan(weights, n_steps, step_of):
    in_specs, out_specs, out_shapes, vmem = [], [], [], 0
    for w in weights:
        rows, width = w.shape
        assert rows % (n_steps * BF16_ROWS) == 0, (w.shape, n_steps)
        slab = (rows // n_steps, width)
        spec = pl.BlockSpec(slab, lambda *idx: (step_of(*idx), 0))
        in_specs.append(spec)
        out_specs.append(spec)
        out_shapes.append(jax.ShapeDtypeStruct(w.shape, BF16))
        vmem += 2 * slab[0] * slab[1] * (4 + 2)
    return in_specs, out_specs, out_shapes, vmem


def _cast_slabs(src_refs, dst_refs):
    for src, dst in zip(src_refs, dst_refs):
        dst[...] = src[...].astype(dst.dtype)


def _norm_proj_kernel(x_ref, g_ref, wt_ref, wlr_ref, o_ref, lr_ref, h_ref):
    @pl.when(pl.program_id(1) == 0)
    def _():
        h = _rms(x_ref[...], g_ref[...]).astype(BF16)
        h_ref[...] = h
        wlr = wlr_ref[...]
        wlr = jnp.concatenate([wlr, jnp.zeros((LANES - wlr.shape[0], wlr.shape[1]), F32)], axis=0)
        lr_ref[...] = _dot_nt(h, wlr.astype(BF16)).astype(lr_ref.dtype)

    o_ref[...] = _dot_nt(h_ref[...], wt_ref[...].astype(BF16)).astype(o_ref.dtype)


def _norm_proj(x2, gain, wt, *, lr0, rank, bm, bn):
    m, d = x2.shape
    n = wt.shape[0] - rank
    assert lr0 % bn == 0 and n % bn == 0 and rank % SUBLANES == 0

    def w_rows(i, j):
        skip = jnp.where(j * bn >= lr0, rank // SUBLANES, 0)
        return ((j * (bn // SUBLANES) + skip) * SUBLANES, 0)

    vmem = 2 * bm * d * 4 + 2 * bn * d * 4 + 2 * bm * bn * 2 + bm * d * 2 + bm * bn * 4 + d * bn * 2
    return pl.pallas_call(
        _norm_proj_kernel,
        out_shape=(jax.ShapeDtypeStruct((m, n), BF16), jax.ShapeDtypeStruct((m, LANES), BF16)),
        grid=(m // bm, n // bn),
        in_specs=[
            pl.BlockSpec((bm, d), lambda i, j: (i, 0)),
            pl.BlockSpec((1, d), lambda i, j: (0, 0)),
            pl.BlockSpec((pl.Element(bn), pl.Element(d)), w_rows),
            pl.BlockSpec((pl.Element(rank), pl.Element(d)), lambda i, j: (lr0, 0)),
        ],
        out_specs=(pl.BlockSpec((bm, bn), lambda i, j: (i, j)),
                   pl.BlockSpec((bm, LANES), lambda i, j: (i, 0))),
        scratch_shapes=[pltpu.VMEM((bm, d), BF16)],
        compiler_params=_params(("parallel", "arbitrary"), vmem + (8 << 20)),
        name="norm_proj",
    )(x2, gain, wt, wt)


def _gla_kernel(*refs, heads, dk, dv, ts, n_cast):
    q_ref, k_ref, v_ref, og_ref, alr_ref, wup_ref, bup_ref, hg_ref = refs[:8]
    o_ref = refs[8 + n_cast]
    state_ref = refs[-1]
    _cast_slabs(refs[8:8 + n_cast], refs[9 + n_cast:9 + 2 * n_cast])

    @pl.when(pl.program_id(1) == 0)
    def _():
        state_ref[...] = jnp.zeros_like(state_ref)

    w_up = wup_ref[...]
    rank = w_up.shape[0]
    w_up = jnp.concatenate([w_up, jnp.zeros((LANES - rank, w_up.shape[1]), F32)], axis=0)
    z = _dot(alr_ref[...], w_up.astype(BF16)) + bup_ref[...]
    log_a = (jnp.minimum(z, 0.0) - jnp.log(1.0 + jnp.exp(-jnp.abs(z)))) * (1.0 / GLA_TAU)

    row = lax.broadcasted_iota(jnp.int32, (ts, ts), 0)
    col = lax.broadcasted_iota(jnp.int32, (ts, ts), 1)
    tri = ((row // CHUNK == col // CHUNK) & (col <= row)).astype(BF16)
    la_hi = log_a.astype(BF16)
    la_lo = (log_a - la_hi.astype(F32)).astype(BF16)
    b = _dot(tri, la_hi) + _dot(tri, la_lo)

    crow = lax.broadcasted_iota(jnp.int32, (CHUNK, CHUNK), 0)
    ccol = lax.broadcasted_iota(jnp.int32, (CHUNK, CHUNK), 1)
    lower = ccol <= crow

    scale = dk ** -0.5
    gain = hg_ref[...]
    states = [state_ref[h] for h in range(heads)]
    for c in range(ts // CHUNK):
        r = slice(c * CHUNK, (c + 1) * CHUNK)
        bc = b[r, :]
        mid = bc[CHUNK // 2:CHUNK // 2 + 1, :]
        last = bc[CHUNK - 1:CHUNK, :]
        e_fwd = jnp.exp(bc - mid)
        e_rev = jnp.exp(mid - bc)
        qc = q_ref[r, :].astype(F32) * scale
        kc = k_ref[r, :].astype(F32)
        qe = qc * e_fwd
        ke = kc * e_rev
        q_f = qe.astype(BF16)
        k_f = ke.astype(BF16)
        q_r = (qc * e_rev).astype(BF16)
        k_r = (kc * e_fwd).astype(BF16)
        k_l = (ke * jnp.exp(last - mid)).astype(BF16)
        q_i = (qe * jnp.exp(mid)).astype(BF16)
        decay = jnp.exp(last)
        for h in range(heads):
            ck = slice(h * dk, (h + 1) * dk)
            cv = slice(h * dv, (h + 1) * dv)
            vh = v_ref[r, cv]
            att = jnp.where(lower, _dot_nt(q_f[:, ck], k_f[:, ck]), _dot_nt(q_r[:, ck], k_r[:, ck]))
            av = _dot(jnp.concatenate([att.astype(BF16), k_l[:, ck].T], axis=0), vh)
            st = states[h]
            o = av[:CHUNK] + _dot(q_i[:, ck], st.astype(BF16))
            dcol = jnp.transpose(jnp.broadcast_to(decay[:, ck], (dk, dk)))
            states[h] = st * jnp.concatenate([dcol] * (dv // dk), axis=1) + av[CHUNK:]
            o = o * lax.rsqrt(jnp.mean(o * o, axis=-1, keepdims=True) + EPS) * gain
            og = og_ref[r, cv].astype(F32)
            o_ref[r, cv] = (o * (og * jax.nn.sigmoid(og))).astype(o_ref.dtype)
    for h in range(heads):
        state_ref[h] = states[h]


def _gla(proj, a_lr, w_up, b_up, head_gain, cast_weights, *, batch, seq, heads, dk, dv, cols, ts):
    m = proj.shape[0]
    ns = seq // ts
    dkt, dvt = heads * dk, heads * dv
    rank_pad = a_lr.shape[1]

    def rows(b, s):
        return b * ns + s

    c_in, c_out, c_shapes, c_vmem = _cast_plan(cast_weights, batch * ns, rows)
    kernel = functools.partial(_gla_kernel, heads=heads, dk=dk, dv=dv, ts=ts, n_cast=len(c_in))
    vmem = (2 * ts * (2 * dkt + 2 * dvt + rank_pad) * 2 + 2 * ts * dvt * 2 + heads * dk * dv * 4
            + c_vmem)
    return pl.pallas_call(
        kernel,
        out_shape=[jax.ShapeDtypeStruct((m, dvt), BF16)] + c_shapes,
        grid=(batch, ns),
        in_specs=[
            pl.BlockSpec((ts, dkt), lambda b, s: (rows(b, s), cols["q"] // dkt)),
            pl.BlockSpec((ts, dkt), lambda b, s: (rows(b, s), cols["k"] // dkt)),
            pl.BlockSpec((ts, dvt), lambda b, s: (rows(b, s), cols["v"] // dvt)),
            pl.BlockSpec((ts, dvt), lambda b, s: (rows(b, s), cols["og"] // dvt)),
            pl.BlockSpec((ts, rank_pad), lambda b, s: (rows(b, s), 0)),
            _resident(w_up.shape),
            _resident(b_up.shape),
            _resident(head_gain.shape),
        ] + c_in,
        out_specs=[pl.BlockSpec((ts, dvt), lambda b, s: (rows(b, s), 0))] + c_out,
        scratch_shapes=[pltpu.VMEM((heads, dk, dv), F32)],
        compiler_params=_params(("arbitrary", "arbitrary"), vmem + (24 << 20)),
        name="gla",
    )(proj, proj, proj, proj, a_lr, w_up, b_up, head_gain, *cast_weights)


def _mix_out_kernel(*refs, tiles_per_seq, n_cast):
    (ax_ref, ab_ref, ac_ref, axp_ref, acp_ref, o_ref, ga_ref, gb_ref, x_ref,
     cw_ref, wa_ref, wb_ref, wm_ref, g_ref) = refs[:14]
    out_ref = refs[14 + n_cast]
    _cast_slabs(refs[14:14 + n_cast], refs[15 + n_cast:])

    u = ac_ref[...].astype(F32) * ax_ref[...].astype(F32)
    u_prev = acp_ref[...].astype(F32) * axp_ref[...].astype(F32)
    first = (pl.program_id(0) % tiles_per_seq) == 0
    u_prev = jnp.where(first, 0.0, u_prev)
    p1 = u_prev[SUBLANES - 1:SUBLANES, :]
    p2 = u_prev[SUBLANES - 2:SUBLANES - 1, :]
    row = lax.broadcasted_iota(jnp.int32, u.shape, 0)
    u1 = jnp.where(row == 0, p1, pltpu.roll(u, 1, 0))
    u2 = jnp.where(row == 0, p2, jnp.where(row == 1, p1, pltpu.roll(u, 2, 0)))
    cw = cw_ref[...]
    y = cw[0:1, :] * u2 + cw[1:2, :] * u1 + cw[2:3, :] * u
    ca = (ab_ref[...].astype(F32) * y).astype(BF16)

    ya = _dot(ca, wa_ref[...])
    yb = _dot(o_ref[...], wb_ref[...])
    mix = jax.nn.sigmoid(ga_ref[...].astype(F32)) * ya + jax.nn.sigmoid(gb_ref[...].astype(F32)) * yb
    m = _dot(mix.astype(BF16), wm_ref[...])
    out_ref[...] = x_ref[...] + _rms(m, g_ref[...])


def _mix_out(proj, o, x2, conv_w, wa, wb, wm, gain, cast_weights, *, seq, cols, bm):
    m, d = x2.shape
    ca = conv_w.shape[1]
    cb = o.shape[1]
    tiles_per_seq = seq // bm
    halo = bm // SUBLANES

    def prev(i):
        return jnp.maximum(i * halo - 1, 0)

    c_in, c_out, c_shapes, c_vmem = _cast_plan(cast_weights, m // bm, lambda i: i)
    kernel = functools.partial(_mix_out_kernel, tiles_per_seq=tiles_per_seq, n_cast=len(c_in))
    vmem = (2 * bm * (3 * ca + cb + 2 * d) * 2 + 4 * bm * d * 4
            + (ca + cb + d) * d * 2 + 4 * bm * d * 4 + c_vmem)
    return pl.pallas_call(
        kernel,
        out_shape=[jax.ShapeDtypeStruct((m, d), F32)] + c_shapes,
        grid=(m // bm,),
        in_specs=[
            pl.BlockSpec((bm, ca), lambda i: (i, cols["a_x"] // ca)),
            pl.BlockSpec((bm, ca), lambda i: (i, cols["a_b"] // ca)),
            pl.BlockSpec((bm, ca), lambda i: (i, cols["a_c"] // ca)),
            pl.BlockSpec((SUBLANES, ca), lambda i: (prev(i), cols["a_x"] // ca)),
            pl.BlockSpec((SUBLANES, ca), lambda i: (prev(i), cols["a_c"] // ca)),
            pl.BlockSpec((bm, cb), lambda i: (i, 0)),
            pl.BlockSpec((bm, d), lambda i: (i, cols["gate_a"] // d)),
            pl.BlockSpec((bm, d), lambda i: (i, cols["gate_b"] // d)),
            pl.BlockSpec((bm, d), lambda i: (i, 0)),
            _resident(conv_w.shape),
            _resident(wa.shape),
            _resident(wb.shape),
            _resident(wm.shape),
            _resident(gain.shape),
        ] + c_in,
        out_specs=[pl.BlockSpec((bm, d), lambda i: (i, 0))] + c_out,
        compiler_params=_params(("arbitrary",), vmem + (8 << 20)),
        name="mix_out",
    )(proj, proj, proj, proj, proj, o, proj, proj, x2, conv_w, wa, wb, wm, gain, *cast_weights)


def _ffn_kernel(x_ref, gpre_ref, wg_ref, wu_ref, wd_ref, gpost_ref, out_ref, h_ref):
    j = pl.program_id(1)

    @pl.when(j == 0)
    def _():
        h_ref[...] = _rms(x_ref[...], gpre_ref[...]).astype(BF16)
        out_ref[...] = jnp.zeros_like(out_ref)

    h = h_ref[...]
    g = _dot(h, wg_ref[...])
    u = _dot(h, wu_ref[...])
    a = (g * jax.nn.sigmoid(g) * u).astype(BF16)
    out_ref[...] += _dot(a, wd_ref[...])

    @pl.when(j == pl.num_programs(1) - 1)
    def _():
        out_ref[...] = x_ref[...] + _rms(out_ref[...], gpost_ref[...])


def _ffn(x2, g_pre, wg, wu, wd, g_post, *, bm, bf):
    m, d = x2.shape
    f = wg.shape[1]
    vmem = 3 * bm * d * 4 + 2 * 3 * d * bf * 2 + bm * d * 2 + 3 * bm * bf * 4
    return pl.pallas_call(
        _ffn_kernel,
        out_shape=jax.ShapeDtypeStruct((m, d), F32),
        grid=(m // bm, f // bf),
        in_specs=[
            pl.BlockSpec((bm, d), lambda i, j: (i, 0), pipeline_mode=pl.Buffered(1)),
            pl.BlockSpec((1, d), lambda i, j: (0, 0)),
            pl.BlockSpec((d, bf), lambda i, j: (0, j)),
            pl.BlockSpec((d, bf), lambda i, j: (0, j)),
            pl.BlockSpec((bf, d), lambda i, j: (j, 0)),
            pl.BlockSpec((1, d), lambda i, j: (0, 0)),
        ],
        out_specs=pl.BlockSpec((bm, d), lambda i, j: (i, 0)),
        scratch_shapes=[pltpu.VMEM((bm, d), BF16)],
        compiler_params=_params(("parallel", "arbitrary"), vmem + (8 << 20)),
        name="ffn",
    )(x2, g_pre, wg, wu, wd, g_post)


def _ple_kernel(x_ref, p_ref, gpre_ref, wpg_ref, wpp_ref, gpost_ref, out_ref):
    x = x_ref[...]
    h = _rms(x, gpre_ref[...]).astype(BF16)
    e = jax.nn.sigmoid(_dot(h, wpg_ref[...])) * _dot(p_ref[...].astype(BF16),
                                                      wpp_ref[...].astype(BF16))
    out_ref[...] = x + _rms(e, gpost_ref[...])


def _ple(x2, p2, g_pre, wpg, wpp, g_post, *, bm):
    m, d = x2.shape
    pd = p2.shape[1]
    vmem = 4 * bm * d * 4 + 2 * bm * pd * 4 + d * d * 2 + pd * d * 4 + 4 * bm * d * 4
    return pl.pallas_call(
        _ple_kernel,
        out_shape=jax.ShapeDtypeStruct((m, d), F32),
        grid=(m // bm,),
        in_specs=[
            pl.BlockSpec((bm, d), lambda i: (i, 0)),
            pl.BlockSpec((bm, pd), lambda i: (i, 0)),
            _resident(g_pre.shape),
            _resident(wpg.shape),
            _resident(wpp.shape),
            _resident(g_post.shape),
        ],
        out_specs=pl.BlockSpec((bm, d), lambda i: (i, 0)),
        compiler_params=_params(("parallel",), vmem + (8 << 20)),
        name="ple",
    )(x2, p2, g_pre, wpg, wpp, g_post)


def _proj_layout(mix_a, dk_total, mix_b, d):
    names = ["a_x", "a_b", "a_c", "q", "k", "v", "og", "gate_a", "gate_b"]
    widths = [mix_a, mix_a, mix_a, dk_total, dk_total, mix_b, mix_b, d, d]
    cols, off = {}, 0
    for nme, wdt in zip(names, widths):
        assert off % wdt == 0, (nme, off, wdt)
        cols[nme] = off
        off += wdt
    return cols, cols["gate_a"]


def kernel(x, p, w_in, conv_w, w_a_out, w_alpha_up, b_alpha_up, gla_head_gain, w_b_out, w_mix_out, g_pre_mix, g_post_mix, g_pre_ffn, g_post_ffn, w_ff_gate, w_ff_up, w_ff_down, g_pre_ple, g_post_ple, w_ple_gate, w_ple_proj):
    batch, seq, d = x.shape
    depth = w_in.shape[0]
    mix_a = conv_w.shape[-1]
    rank, dk_total = w_alpha_up.shape[1], w_alpha_up.shape[2]
    dv = gla_head_gain.shape[-1]
    mix_b = w_b_out.shape[1]
    heads = mix_b // dv
    dk = dk_total // heads
    assert rank <= LANES and seq % CHUNK == 0
    cols, lr0 = _proj_layout(mix_a, dk_total, mix_b, d)

    xs = x.reshape(batch * seq, d)
    for i in range(depth):
        proj, a_lr = _norm_proj(xs, g_pre_mix[i][None], w_in[i].T, lr0=lr0, rank=rank,
                                bm=1024, bn=1024)
        o, wa, wb, wm = _gla(
            proj, a_lr, w_alpha_up[i], b_alpha_up[i][None], gla_head_gain[i][None],
            [w_a_out[i], w_b_out[i], w_mix_out[i]],
            batch=batch, seq=seq, heads=heads, dk=dk, dv=dv, cols=cols, ts=256)
        xs, wg, wu, wd, wpg = _mix_out(
            proj, o, xs, conv_w[i], wa, wb, wm, g_post_mix[i][None],
            [w_ff_gate[i], w_ff_up[i], w_ff_down[i], w_ple_gate[i]], seq=seq, cols=cols, bm=256)
        xs = _ffn(xs, g_pre_ffn[i][None], wg, wu, wd, g_post_ffn[i][None], bm=1024, bf=512)
        xs = _ple(xs, p[i].reshape(batch * seq, -1), g_pre_ple[i][None], wpg, w_ple_proj[i],
                  g_post_ple[i][None], bm=512)
    return xs.reshape(batch, seq, d)
```

```python
import functools

import jax
import jax.numpy as jnp
from jax import lax
from jax.experimental import pallas as pl
from jax.experimental.pallas import tpu as pltpu

EPS = 1e-6
CHUNK = 64
GLA_TAU = 16.0
CONV_WIDTH = 3
LANES = 128
SUBLANES = 8
VMEM_LIMIT_CAP = 60000 * 1024

F32 = jnp.float32
BF16 = jnp.bfloat16


def _rms(x, gain):
    ms = jnp.mean(x * x, axis=-1, keepdims=True)
    return x * lax.rsqrt(ms + EPS) * gain


def _dot(a, b):
    return jnp.dot(a, b, preferred_element_type=F32)


def _dot_nt(a, b):
    return lax.dot_general(a, b, (((1,), (1,)), ((), ())), preferred_element_type=F32)


def _dot_tn(a, b):
    return lax.dot_general(a, b, (((0,), (0,)), ((), ())), preferred_element_type=F32)


def _params(semantics, vmem_bytes):
    return pltpu.CompilerParams(
        dimension_semantics=semantics,
        vmem_limit_bytes=min(int(vmem_bytes), VMEM_LIMIT_CAP),
    )


def _resident(shape):
    return pl.BlockSpec(shape, lambda *_: (0,) * len(shape), pipeline_mode=pl.Buffered(1))


BF16_ROWS = 16


def _cast_plan(weights, n_steps, step_of):
    in_specs, out_specs, out_shapes, vmem = [], [], [], 0
    for w in weights:
        rows, width = w.shape
        assert rows % (n_steps * BF16_ROWS) == 0, (w.shape, n_steps)
        slab = (rows // n_steps, width)
        spec = pl.BlockSpec(slab, lambda *idx: (step_of(*idx), 0))
        in_specs.append(spec)
        out_specs.append(spec)
        out_shapes.append(jax.ShapeDtypeStruct(w.shape, BF16))
        vmem += 2 * slab[0] * slab[1] * (4 + 2)
    return in_specs, out_specs, out_shapes, vmem


def _cast_slabs(src_refs, dst_refs):
    for src, dst in zip(src_refs, dst_refs):
        dst[...] = src[...].astype(dst.dtype)


def _norm_proj_kernel(x_ref, g_ref, wt_ref, wlr_ref, o_ref, lr_ref, h_ref):
    @pl.when(pl.program_id(1) == 0)
    def _():
        h = _rms(x_ref[...], g_ref[...]).astype(BF16)
        h_ref[...] = h
        wlr = wlr_ref[...]
        wlr = jnp.concatenate([wlr, jnp.zeros((LANES - wlr.shape[0], wlr.shape[1]), F32)], axis=0)
        lr_ref[...] = _dot_nt(h, wlr.astype(BF16)).astype(lr_ref.dtype)

    o_ref[...] = _dot_nt(h_ref[...], wt_ref[...].astype(BF16)).astype(o_ref.dtype)


def _norm_proj(x2, gain, wt, *, lr0, rank, bm, bn):
    m, d = x2.shape
    n = wt.shape[0] - rank
    assert lr0 % bn == 0 and n % bn == 0 and rank % SUBLANES == 0

    def w_rows(i, j):
        skip = jnp.where(j * bn >= lr0, rank // SUBLANES, 0)
        return ((j * (bn // SUBLANES) + skip) * SUBLANES, 0)

    vmem = 2 * bm * d * 4 + 2 * bn * d * 4 + 2 * bm * bn * 2 + bm * d * 2 + bm * bn * 4 + d * bn * 2
    return pl.pallas_call(
        _norm_proj_kernel,
        out_shape=(jax.ShapeDtypeStruct((m, n), BF16), jax.ShapeDtypeStruct((m, LANES), BF16)),
        grid=(m // bm, n // bn),
        in_specs=[
            pl.BlockSpec((bm, d), lambda i, j: (i, 0)),
            pl.BlockSpec((1, d), lambda i, j: (0, 0)),
            pl.BlockSpec((pl.Element(bn), pl.Element(d)), w_rows),
            pl.BlockSpec((pl.Element(rank), pl.Element(d)), lambda i, j: (lr0, 0)),
        ],
        out_specs=(pl.BlockSpec((bm, bn), lambda i, j: (i, j)),
                   pl.BlockSpec((bm, LANES), lambda i, j: (i, 0))),
        scratch_shapes=[pltpu.VMEM((bm, d), BF16)],
        compiler_params=_params(("parallel", "arbitrary"), vmem + (8 << 20)),
        name="norm_proj",
    )(x2, gain, wt, wt)


def _gla_kernel(*refs, heads, dk, dv, ts, n_cast):
    q_ref, k_ref, v_ref, og_ref, alr_ref, wup_ref, bup_ref, hg_ref = refs[:8]
    o_ref = refs[8 + n_cast]
    state_ref = refs[-1]
    _cast_slabs(refs[8:8 + n_cast], refs[9 + n_cast:9 + 2 * n_cast])

    @pl.when(pl.program_id(1) == 0)
    def _():
        state_ref[...] = jnp.zeros_like(state_ref)

    w_up = wup_ref[...]
    rank = w_up.shape[0]
    w_up = jnp.concatenate([w_up, jnp.zeros((LANES - rank, w_up.shape[1]), F32)], axis=0)
    z = _dot(alr_ref[...], w_up.astype(BF16)) + bup_ref[...]
    log_a = (jnp.minimum(z, 0.0) - jnp.log(1.0 + jnp.exp(-jnp.abs(z)))) * (1.0 / GLA_TAU)

    row = lax.broadcasted_iota(jnp.int32, (ts, ts), 0)
    col = lax.broadcasted_iota(jnp.int32, (ts, ts), 1)
    tri = ((row // CHUNK == col // CHUNK) & (col <= row)).astype(BF16)
    la_hi = log_a.astype(BF16)
    la_lo = (log_a - la_hi.astype(F32)).astype(BF16)
    b = _dot(tri, la_hi) + _dot(tri, la_lo)

    crow = lax.broadcasted_iota(jnp.int32, (CHUNK, CHUNK), 0)
    ccol = lax.broadcasted_iota(jnp.int32, (CHUNK, CHUNK), 1)
    lower = ccol <= crow

    scale = dk ** -0.5
    gain = hg_ref[...]
    states = [state_ref[h] for h in range(heads)]
    for c in range(ts // CHUNK):
        r = slice(c * CHUNK, (c + 1) * CHUNK)
        bc = b[r, :]
        mid = bc[CHUNK // 2:CHUNK // 2 + 1, :]
        last = bc[CHUNK - 1:CHUNK, :]
        e_fwd = jnp.exp(bc - mid)
        e_rev = jnp.exp(mid - bc)
        qc = q_ref[r, :].astype(F32) * scale
        kc = k_ref[r, :].astype(F32)
        qe = qc * e_fwd
        ke = kc * e_rev
        q_f = qe.astype(BF16)
        k_f = ke.astype(BF16)
        q_r = (qc * e_rev).astype(BF16)
        k_r = (kc * e_fwd).astype(BF16)
        k_l = (ke * jnp.exp(last - mid)).astype(BF16)
        q_i = (qe * jnp.exp(mid)).astype(BF16)
        decay = jnp.exp(last)
        for h in range(heads):
            ck = slice(h * dk, (h + 1) * dk)
            cv = slice(h * dv, (h + 1) * dv)
            vh = v_ref[r, cv]
            att = jnp.where(lower, _dot_nt(q_f[:, ck], k_f[:, ck]), _dot_nt(q_r[:, ck], k_r[:, ck]))
            av = _dot(jnp.concatenate([att.astype(BF16), k_l[:, ck].T], axis=0), vh)
            st = states[h]
            o = av[:CHUNK] + _dot(q_i[:, ck], st.astype(BF16))
            dcol = jnp.transpose(jnp.broadcast_to(decay[:, ck], (dk, dk)))
            states[h] = st * jnp.concatenate([dcol] * (dv // dk), axis=1) + av[CHUNK:]
            o = o * lax.rsqrt(jnp.mean(o * o, axis=-1, keepdims=True) + EPS) * gain
            og = og_ref[r, cv].astype(F32)
            o_ref[r, cv] = (o * (og * jax.nn.sigmoid(og))).astype(o_ref.dtype)
    for h in range(heads):
        state_ref[h] = states[h]


def _gla(proj, a_lr, w_up, b_up, head_gain, cast_weights, *, batch, seq, heads, dk, dv, cols, ts):
    m = proj.shape[0]
    ns = seq // ts
    dkt, dvt = heads * dk, heads * dv
    rank_pad = a_lr.shape[1]

    def rows(b, s):
        return b * ns + s

    c_in, c_out, c_shapes, c_vmem = _cast_plan(cast_weights, batch * ns, rows)
    kernel = functools.partial(_gla_kernel, heads=heads, dk=dk, dv=dv, ts=ts, n_cast=len(c_in))
    vmem = (2 * ts * (2 * dkt + 2 * dvt + rank_pad) * 2 + 2 * ts * dvt * 2 + heads * dk * dv * 4
            + c_vmem)
    return pl.pallas_call(
        kernel,
        out_shape=[jax.ShapeDtypeStruct((m, dvt), BF16)] + c_shapes,
        grid=(batch, ns),
        in_specs=[
            pl.BlockSpec((ts, dkt), lambda b, s: (rows(b, s), cols["q"] // dkt)),
            pl.BlockSpec((ts, dkt), lambda b, s: (rows(b, s), cols["k"] // dkt)),
            pl.BlockSpec((ts, dvt), lambda b, s: (rows(b, s), cols["v"] // dvt)),
            pl.BlockSpec((ts, dvt), lambda b, s: (rows(b, s), cols["og"] // dvt)),
            pl.BlockSpec((ts, rank_pad), lambda b, s: (rows(b, s), 0)),
            _resident(w_up.shape),
            _resident(b_up.shape),
            _resident(head_gain.shape),
        ] + c_in,
        out_specs=[pl.BlockSpec((ts, dvt), lambda b, s: (rows(b, s), 0))] + c_out,
        scratch_shapes=[pltpu.VMEM((heads, dk, dv), F32)],
        compiler_params=_params(("arbitrary", "arbitrary"), vmem + (24 << 20)),
        name="gla",
    )(proj, proj, proj, proj, a_lr, w_up, b_up, head_gain, *cast_weights)


def _mix_out_kernel(*refs, tiles_per_seq, n_cast):
    (ax_ref, ab_ref, ac_ref, axp_ref, acp_ref, o_ref, ga_ref, gb_ref, x_ref,
     cw_ref, wa_ref, wb_ref, wm_ref, g_ref, gnext_ref) = refs[:15]
    out_ref, hnext_ref = refs[15 + n_cast:17 + n_cast]
    _cast_slabs(refs[15:15 + n_cast], refs[17 + n_cast:])

    u = ac_ref[...].astype(F32) * ax_ref[...].astype(F32)
    u_prev = acp_ref[...].astype(F32) * axp_ref[...].astype(F32)
    first = (pl.program_id(0) % tiles_per_seq) == 0
    u_prev = jnp.where(first, 0.0, u_prev)
    p1 = u_prev[SUBLANES - 1:SUBLANES, :]
    p2 = u_prev[SUBLANES - 2:SUBLANES - 1, :]
    row = lax.broadcasted_iota(jnp.int32, u.shape, 0)
    u1 = jnp.where(row == 0, p1, pltpu.roll(u, 1, 0))
    u2 = jnp.where(row == 0, p2, jnp.where(row == 1, p1, pltpu.roll(u, 2, 0)))
    cw = cw_ref[...]
    y = cw[0:1, :] * u2 + cw[1:2, :] * u1 + cw[2:3, :] * u
    ca = (ab_ref[...].astype(F32) * y).astype(BF16)

    ya = _dot(ca, wa_ref[...])
    yb = _dot(o_ref[...], wb_ref[...])
    mix = jax.nn.sigmoid(ga_ref[...].astype(F32)) * ya + jax.nn.sigmoid(gb_ref[...].astype(F32)) * yb
    m = _dot(mix.astype(BF16), wm_ref[...])
    x_new = x_ref[...] + _rms(m, g_ref[...])
    out_ref[...] = x_new
    hnext_ref[...] = _rms(x_new, gnext_ref[...]).astype(hnext_ref.dtype)


def _mix_out(proj, o, x2, conv_w, wa, wb, wm, gain, gain_next, cast_weights, *, seq, cols, bm):
    m, d = x2.shape
    ca = conv_w.shape[1]
    cb = o.shape[1]
    tiles_per_seq = seq // bm
    halo = bm // SUBLANES

    def prev(i):
        return jnp.maximum(i * halo - 1, 0)

    c_in, c_out, c_shapes, c_vmem = _cast_plan(cast_weights, m // bm, lambda i: i)
    kernel = functools.partial(_mix_out_kernel, tiles_per_seq=tiles_per_seq, n_cast=len(c_in))
    vmem = (2 * bm * (3 * ca + cb + 3 * d) * 2 + 4 * bm * d * 4
            + (ca + cb + d) * d * 2 + 4 * bm * d * 4 + c_vmem)
    return pl.pallas_call(
        kernel,
        out_shape=[jax.ShapeDtypeStruct((m, d), F32), jax.ShapeDtypeStruct((m, d), BF16)] + c_shapes,
        grid=(m // bm,),
        in_specs=[
            pl.BlockSpec((bm, ca), lambda i: (i, cols["a_x"] // ca)),
            pl.BlockSpec((bm, ca), lambda i: (i, cols["a_b"] // ca)),
            pl.BlockSpec((bm, ca), lambda i: (i, cols["a_c"] // ca)),
            pl.BlockSpec((SUBLANES, ca), lambda i: (prev(i), cols["a_x"] // ca)),
            pl.BlockSpec((SUBLANES, ca), lambda i: (prev(i), cols["a_c"] // ca)),
            pl.BlockSpec((bm, cb), lambda i: (i, 0)),
            pl.BlockSpec((bm, d), lambda i: (i, cols["gate_a"] // d)),
            pl.BlockSpec((bm, d), lambda i: (i, cols["gate_b"] // d)),
            pl.BlockSpec((bm, d), lambda i: (i, 0)),
            _resident(conv_w.shape),
            _resident(wa.shape),
            _resident(wb.shape),
            _resident(wm.shape),
            _resident(gain.shape),
            _resident(gain_next.shape),
        ] + c_in,
        out_specs=[pl.BlockSpec((bm, d), lambda i: (i, 0))] * 2 + c_out,
        compiler_params=_params(("arbitrary",), vmem + (8 << 20)),
        name="mix_out",
    )(proj, proj, proj, proj, proj, o, proj, proj, x2, conv_w, wa, wb, wm, gain, gain_next,
      *cast_weights)


def _ffn_kernel(h_ref, wg_ref, wu_ref, wd_ref, gpost_ref, out_ref):
    j = pl.program_id(1)

    @pl.when(j == 0)
    def _():
        out_ref[...] = jnp.zeros_like(out_ref)

    h = h_ref[...]
    g = _dot(h, wg_ref[...])
    u = _dot(h, wu_ref[...])
    a = (g * jax.nn.sigmoid(g) * u).astype(BF16)
    out_ref[...] += _dot(a, wd_ref[...])

    @pl.when(j == pl.num_programs(1) - 1)
    def _():
        out_ref[...] = _rms(out_ref[...], gpost_ref[...])


def _ffn(h, wg, wu, wd, g_post, *, bm, bf):
    m, d = h.shape
    f = wg.shape[1]
    vmem = 2 * bm * d * 2 + 2 * bm * d * 4 + 2 * 3 * d * bf * 2 + 3 * bm * bf * 4
    return pl.pallas_call(
        _ffn_kernel,
        out_shape=jax.ShapeDtypeStruct((m, d), F32),
        grid=(m // bm, f // bf),
        in_specs=[
            pl.BlockSpec((bm, d), lambda i, j: (i, 0)),
            pl.BlockSpec((d, bf), lambda i, j: (0, j)),
            pl.BlockSpec((d, bf), lambda i, j: (0, j)),
            pl.BlockSpec((bf, d), lambda i, j: (j, 0)),
            pl.BlockSpec((1, d), lambda i, j: (0, 0)),
        ],
        out_specs=pl.BlockSpec((bm, d), lambda i, j: (i, 0)),
        compiler_params=_params(("parallel", "arbitrary"), vmem + (8 << 20)),
        name="ffn",
    )(h, wg, wu, wd, g_post)


def _ple_kernel(x_ref, f_ref, p_ref, gpre_ref, wpg_ref, wpp_ref, gpost_ref, out_ref):
    x = x_ref[...] + f_ref[...]
    h = _rms(x, gpre_ref[...]).astype(BF16)
    e = jax.nn.sigmoid(_dot(h, wpg_ref[...])) * _dot(p_ref[...].astype(BF16),
                                                      wpp_ref[...].astype(BF16))
    out_ref[...] = x + _rms(e, gpost_ref[...])


def _ple(x2, f2, p2, g_pre, wpg, wpp, g_post, *, bm):
    m, d = x2.shape
    pd = p2.shape[1]
    vmem = 6 * bm * d * 4 + 2 * bm * pd * 4 + d * d * 2 + pd * d * 4 + 4 * bm * d * 4
    return pl.pallas_call(
        _ple_kernel,
        out_shape=jax.ShapeDtypeStruct((m, d), F32),
        grid=(m // bm,),
        in_specs=[
            pl.BlockSpec((bm, d), lambda i: (i, 0)),
            pl.BlockSpec((bm, d), lambda i: (i, 0)),
            pl.BlockSpec((bm, pd), lambda i: (i, 0)),
            _resident(g_pre.shape),
            _resident(wpg.shape),
            _resident(wpp.shape),
            _resident(g_post.shape),
        ],
        out_specs=pl.BlockSpec((bm, d), lambda i: (i, 0)),
        compiler_params=_params(("parallel",), vmem + (8 << 20)),
        name="ple",
    )(x2, f2, p2, g_pre, wpg, wpp, g_post)


def _proj_layout(mix_a, dk_total, mix_b, d):
    names = ["a_x", "a_b", "a_c", "q", "k", "v", "og", "gate_a", "gate_b"]
    widths = [mix_a, mix_a, mix_a, dk_total, dk_total, mix_b, mix_b, d, d]
    cols, off = {}, 0
    for nme, wdt in zip(names, widths):
        assert off % wdt == 0, (nme, off, wdt)
        cols[nme] = off
        off += wdt
    return cols, cols["gate_a"]


def kernel(x, p, w_in, conv_w, w_a_out, w_alpha_up, b_alpha_up, gla_head_gain, w_b_out, w_mix_out, g_pre_mix, g_post_mix, g_pre_ffn, g_post_ffn, w_ff_gate, w_ff_up, w_ff_down, g_pre_ple, g_post_ple, w_ple_gate, w_ple_proj):
    batch, seq, d = x.shape
    depth = w_in.shape[0]
    mix_a = conv_w.shape[-1]
    rank, dk_total = w_alpha_up.shape[1], w_alpha_up.shape[2]
    dv = gla_head_gain.shape[-1]
    mix_b = w_b_out.shape[1]
    heads = mix_b // dv
    dk = dk_total // heads
    assert rank <= LANES and seq % CHUNK == 0
    cols, lr0 = _proj_layout(mix_a, dk_total, mix_b, d)

    xs = x.reshape(batch * seq, d)
    for i in range(depth):
        proj, a_lr = _norm_proj(xs, g_pre_mix[i][None], w_in[i].T, lr0=lr0, rank=rank,
                                bm=1024, bn=1024)
        o, wa, wb, wm = _gla(
            proj, a_lr, w_alpha_up[i], b_alpha_up[i][None], gla_head_gain[i][None],
            [w_a_out[i], w_b_out[i], w_mix_out[i]],
            batch=batch, seq=seq, heads=heads, dk=dk, dv=dv, cols=cols, ts=256)
        xs, h_ffn, wg, wu, wd, wpg = _mix_out(
            proj, o, xs, conv_w[i], wa, wb, wm, g_post_mix[i][None], g_pre_ffn[i][None],
            [w_ff_gate[i], w_ff_up[i], w_ff_down[i], w_ple_gate[i]], seq=seq, cols=cols, bm=256)
        f_norm = _ffn(h_ffn, wg, wu, wd, g_post_ffn[i][None], bm=1024, bf=512)
        xs = _ple(xs, f_norm, p[i].reshape(batch * seq, -1), g_pre_ple[i][None], wpg,
                  w_ple_proj[i], g_post_ple[i][None], bm=512)
    return xs.reshape(batch, seq, d)
```

```python
import functools

import jax
import jax.numpy as jnp
from jax import lax
from jax.experimental import pallas as pl
from jax.experimental.pallas import tpu as pltpu

EPS = 1e-6
CHUNK = 64
GLA_TAU = 16.0
CONV_WIDTH = 3
LANES = 128
SUBLANES = 8
VMEM_LIMIT_CAP = 60000 * 1024

F32 = jnp.float32
BF16 = jnp.bfloat16


def _rms(x, gain):
    ms = jnp.mean(x * x, axis=-1, keepdims=True)
    return x * lax.rsqrt(ms + EPS) * gain


def _dot(a, b):
    return jnp.dot(a, b, preferred_element_type=F32)


def _dot_nt(a, b):
    return lax.dot_general(a, b, (((1,), (1,)), ((), ())), preferred_element_type=F32)


def _dot_tn(a, b):
    return lax.dot_general(a, b, (((0,), (0,)), ((), ())), preferred_element_type=F32)


def _params(semantics, vmem_bytes):
    return pltpu.CompilerParams(
        dimension_semantics=semantics,
        vmem_limit_bytes=min(int(vmem_bytes), VMEM_LIMIT_CAP),
    )


def _resident(shape):
    return pl.BlockSpec(shape, lambda *_: (0,) * len(shape), pipeline_mode=pl.Buffered(1))


BF16_ROWS = 16


def _cast_plan(weights, n_steps, step_of):
    in_specs, out_specs, out_shapes, vmem = [], [], [], 0
    for w in weights:
        rows, width = w.shape
        assert rows % (n_steps * BF16_ROWS) == 0, (w.shape, n_steps)
        slab = (rows // n_steps, width)
        spec = pl.BlockSpec(slab, lambda *idx: (step_of(*idx), 0))
        in_specs.append(spec)
        out_specs.append(spec)
        out_shapes.append(jax.ShapeDtypeStruct(w.shape, BF16))
        vmem += 2 * slab[0] * slab[1] * (4 + 2)
    return in_specs, out_specs, out_shapes, vmem


def _cast_slabs(src_refs, dst_refs):
    for src, dst in zip(src_refs, dst_refs):
        dst[...] = src[...].astype(dst.dtype)


def _norm_proj_kernel(x_ref, g_ref, wt_ref, wlr_ref, o_ref, lr_ref, h_ref):
    @pl.when(pl.program_id(1) == 0)
    def _():
        h = _rms(x_ref[...], g_ref[...]).astype(BF16)
        h_ref[...] = h
        wlr = wlr_ref[...]
        wlr = jnp.concatenate([wlr, jnp.zeros((LANES - wlr.shape[0], wlr.shape[1]), F32)], axis=0)
        lr_ref[...] = _dot_nt(h, wlr.astype(BF16)).astype(lr_ref.dtype)

    o_ref[...] = _dot_nt(h_ref[...], wt_ref[...].astype(BF16)).astype(o_ref.dtype)


def _norm_proj(x2, gain, wt, *, lr0, rank, bm, bn):
    m, d = x2.shape
    n = wt.shape[0] - rank
    assert lr0 % bn == 0 and n % bn == 0 and rank % SUBLANES == 0

    def w_rows(i, j):
        skip = jnp.where(j * bn >= lr0, rank // SUBLANES, 0)
        return ((j * (bn // SUBLANES) + skip) * SUBLANES, 0)

    vmem = 2 * bm * d * 4 + 2 * bn * d * 4 + 2 * bm * bn * 2 + bm * d * 2 + bm * bn * 4 + d * bn * 2
    return pl.pallas_call(
        _norm_proj_kernel,
        out_shape=(jax.ShapeDtypeStruct((m, n), BF16), jax.ShapeDtypeStruct((m, LANES), BF16)),
        grid=(m // bm, n // bn),
        in_specs=[
            pl.BlockSpec((bm, d), lambda i, j: (i, 0)),
            pl.BlockSpec((1, d), lambda i, j: (0, 0)),
            pl.BlockSpec((pl.Element(bn), pl.Element(d)), w_rows),
            pl.BlockSpec((pl.Element(rank), pl.Element(d)), lambda i, j: (lr0, 0)),
        ],
        out_specs=(pl.BlockSpec((bm, bn), lambda i, j: (i, j)),
                   pl.BlockSpec((bm, LANES), lambda i, j: (i, 0))),
        scratch_shapes=[pltpu.VMEM((bm, d), BF16)],
        compiler_params=_params(("parallel", "arbitrary"), vmem + (8 << 20)),
        name="norm_proj",
    )(x2, gain, wt, wt)


def _gla_kernel(*refs, heads, dk, dv, ts, n_cast):
    q_ref, k_ref, v_ref, og_ref, alr_ref, wup_ref, bup_ref, hg_ref = refs[:8]
    o_ref = refs[8 + n_cast]
    state_ref = refs[-1]
    _cast_slabs(refs[8:8 + n_cast], refs[9 + n_cast:9 + 2 * n_cast])

    @pl.when(pl.program_id(1) == 0)
    def _():
        state_ref[...] = jnp.zeros_like(state_ref)

    w_up = wup_ref[...]
    rank = w_up.shape[0]
    w_up = jnp.concatenate([w_up, jnp.zeros((LANES - rank, w_up.shape[1]), F32)], axis=0)
    z = _dot(alr_ref[...], w_up.astype(BF16)) + bup_ref[...]
    log_a = (jnp.minimum(z, 0.0) - jnp.log(1.0 + jnp.exp(-jnp.abs(z)))) * (1.0 / GLA_TAU)

    row = lax.broadcasted_iota(jnp.int32, (ts, ts), 0)
    col = lax.broadcasted_iota(jnp.int32, (ts, ts), 1)
    tri = ((row // CHUNK == col // CHUNK) & (col <= row)).astype(BF16)
    la_hi = log_a.astype(BF16)
    la_lo = (log_a - la_hi.astype(F32)).astype(BF16)
    b = _dot(tri, la_hi) + _dot(tri, la_lo)

    crow = lax.broadcasted_iota(jnp.int32, (CHUNK, CHUNK), 0)
    ccol = lax.broadcasted_iota(jnp.int32, (CHUNK, CHUNK), 1)
    lower = ccol <= crow

    scale = dk ** -0.5
    gain = hg_ref[...]
    states = [state_ref[h] for h in range(heads)]
    for c in range(ts // CHUNK):
        r = slice(c * CHUNK, (c + 1) * CHUNK)
        bc = b[r, :]
        mid = bc[CHUNK // 2:CHUNK // 2 + 1, :]
        last = bc[CHUNK - 1:CHUNK, :]
        e_fwd = jnp.exp(bc - mid)
        e_rev = jnp.exp(mid - bc)
        qc = q_ref[r, :].astype(F32) * scale
        kc = k_ref[r, :].astype(F32)
        qe = qc * e_fwd
        ke = kc * e_rev
        q_f = qe.astype(BF16)
        k_f = ke.astype(BF16)
        q_r = (qc * e_rev).astype(BF16)
        k_r = (kc * e_fwd).astype(BF16)
        k_l = (ke * jnp.exp(last - mid)).astype(BF16)
        q_i = (qe * jnp.exp(mid)).astype(BF16)
        decay = jnp.exp(last)
        for h in range(heads):
            ck = slice(h * dk, (h + 1) * dk)
            cv = slice(h * dv, (h + 1) * dv)
            vh = v_ref[r, cv]
            att = jnp.where(lower, _dot_nt(q_f[:, ck], k_f[:, ck]), _dot_nt(q_r[:, ck], k_r[:, ck]))
            av = _dot(jnp.concatenate([att.astype(BF16), k_l[:, ck].T], axis=0), vh)
            st = states[h]
            o = av[:CHUNK] + _dot(q_i[:, ck], st.astype(BF16))
            dcol = jnp.transpose(jnp.broadcast_to(decay[:, ck], (dk, dk)))
            states[h] = st * jnp.concatenate([dcol] * (dv // dk), axis=1) + av[CHUNK:]
            o = o * lax.rsqrt(jnp.mean(o * o, axis=-1, keepdims=True) + EPS) * gain
            og = og_ref[r, cv].astype(F32)
            o_ref[r, cv] = (o * (og * jax.nn.sigmoid(og))).astype(o_ref.dtype)
    for h in range(heads):
        state_ref[h] = states[h]


def _gla(proj, a_lr, w_up, b_up, head_gain, cast_weights, *, batch, seq, heads, dk, dv, cols, ts):
    m = proj.shape[0]
    ns = seq // ts
    dkt, dvt = heads * dk, heads * dv
    rank_pad = a_lr.shape[1]

    def rows(b, s):
        return b * ns + s

    c_in, c_out, c_shapes, c_vmem = _cast_plan(cast_weights, batch * ns, rows)
    kernel = functools.partial(_gla_kernel, heads=heads, dk=dk, dv=dv, ts=ts, n_cast=len(c_in))
    vmem = (2 * ts * (2 * dkt + 2 * dvt + rank_pad) * 2 + 2 * ts * dvt * 2 + heads * dk * dv * 4
            + c_vmem)
    return pl.pallas_call(
        kernel,
        out_shape=[jax.ShapeDtypeStruct((m, dvt), BF16)] + c_shapes,
        grid=(batch, ns),
        in_specs=[
            pl.BlockSpec((ts, dkt), lambda b, s: (rows(b, s), cols["q"] // dkt)),
            pl.BlockSpec((ts, dkt), lambda b, s: (rows(b, s), cols["k"] // dkt)),
            pl.BlockSpec((ts, dvt), lambda b, s: (rows(b, s), cols["v"] // dvt)),
            pl.BlockSpec((ts, dvt), lambda b, s: (rows(b, s), cols["og"] // dvt)),
            pl.BlockSpec((ts, rank_pad), lambda b, s: (rows(b, s), 0)),
            _resident(w_up.shape),
            _resident(b_up.shape),
            _resident(head_gain.shape),
        ] + c_in,
        out_specs=[pl.BlockSpec((ts, dvt), lambda b, s: (rows(b, s), 0))] + c_out,
        scratch_shapes=[pltpu.VMEM((heads, dk, dv), F32)],
        compiler_params=_params(("arbitrary", "arbitrary"), vmem + (24 << 20)),
        name="gla",
    )(proj, proj, proj, proj, a_lr, w_up, b_up, head_gain, *cast_weights)


def _mix_out_kernel(*refs, tiles_per_seq, n_cast):
    (ax_ref, ab_ref, ac_ref, axp_ref, acp_ref, o_ref, ga_ref, gb_ref, x_ref,
     cw_ref, wa_ref, wb_ref, wm_ref, g_ref, gnext_ref) = refs[:15]
    out_ref, hnext_ref = refs[15 + n_cast:17 + n_cast]
    _cast_slabs(refs[15:15 + n_cast], refs[17 + n_cast:])

    u = ac_ref[...].astype(F32) * ax_ref[...].astype(F32)
    u_prev = acp_ref[...].astype(F32) * axp_ref[...].astype(F32)
    first = (pl.program_id(0) % tiles_per_seq) == 0
    u_prev = jnp.where(first, 0.0, u_prev)
    p1 = u_prev[SUBLANES - 1:SUBLANES, :]
    p2 = u_prev[SUBLANES - 2:SUBLANES - 1, :]
    row = lax.broadcasted_iota(jnp.int32, u.shape, 0)
    u1 = jnp.where(row == 0, p1, pltpu.roll(u, 1, 0))
    u2 = jnp.where(row == 0, p2, jnp.where(row == 1, p1, pltpu.roll(u, 2, 0)))
    cw = cw_ref[...]
    y = cw[0:1, :] * u2 + cw[1:2, :] * u1 + cw[2:3, :] * u
    ca = (ab_ref[...].astype(F32) * y).astype(BF16)

    ya = _dot(ca, wa_ref[...])
    yb = _dot(o_ref[...], wb_ref[...])
    mix = jax.nn.sigmoid(ga_ref[...].astype(F32)) * ya + jax.nn.sigmoid(gb_ref[...].astype(F32)) * yb
    m = _dot(mix.astype(BF16), wm_ref[...])
    x_new = x_ref[...] + _rms(m, g_ref[...])
    out_ref[...] = x_new
    hnext_ref[...] = _rms(x_new, gnext_ref[...]).astype(hnext_ref.dtype)


def _mix_out(proj, o, x2, conv_w, wa, wb, wm, gain, gain_next, cast_weights, *, seq, cols, bm):
    m, d = x2.shape
    ca = conv_w.shape[1]
    cb = o.shape[1]
    tiles_per_seq = seq // bm
    halo = bm // SUBLANES

    def prev(i):
        return jnp.maximum(i * halo - 1, 0)

    c_in, c_out, c_shapes, c_vmem = _cast_plan(cast_weights, m // bm, lambda i: i)
    kernel = functools.partial(_mix_out_kernel, tiles_per_seq=tiles_per_seq, n_cast=len(c_in))
    vmem = (2 * bm * (3 * ca + cb + 3 * d) * 2 + 4 * bm * d * 4
            + (ca + cb + d) * d * 2 + 4 * bm * d * 4 + c_vmem)
    return pl.pallas_call(
        kernel,
        out_shape=[jax.ShapeDtypeStruct((m, d), F32), jax.ShapeDtypeStruct((m, d), BF16)] + c_shapes,
        grid=(m // bm,),
        in_specs=[
            pl.BlockSpec((bm, ca), lambda i: (i, cols["a_x"] // ca)),
            pl.BlockSpec((bm, ca), lambda i: (i, cols["a_b"] // ca)),
            pl.BlockSpec((bm, ca), lambda i: (i, cols["a_c"] // ca)),
            pl.BlockSpec((SUBLANES, ca), lambda i: (prev(i), cols["a_x"] // ca)),
            pl.BlockSpec((SUBLANES, ca), lambda i: (prev(i), cols["a_c"] // ca)),
            pl.BlockSpec((bm, cb), lambda i: (i, 0)),
            pl.BlockSpec((bm, d), lambda i: (i, cols["gate_a"] // d)),
            pl.BlockSpec((bm, d), lambda i: (i, cols["gate_b"] // d)),
            pl.BlockSpec((bm, d), lambda i: (i, 0)),
            _resident(conv_w.shape),
            _resident(wa.shape),
            _resident(wb.shape),
            _resident(wm.shape),
            _resident(gain.shape),
            _resident(gain_next.shape),
        ] + c_in,
        out_specs=[pl.BlockSpec((bm, d), lambda i: (i, 0))] * 2 + c_out,
        compiler_params=_params(("arbitrary",), vmem + (8 << 20)),
        name="mix_out",
    )(proj, proj, proj, proj, proj, o, proj, proj, x2, conv_w, wa, wb, wm, gain, gain_next,
      *cast_weights)


def _ffn_kernel(h_ref, wg_ref, wu_ref, wd_ref, gpost_ref, out_ref):
    j = pl.program_id(1)

    @pl.when(j == 0)
    def _():
        out_ref[...] = jnp.zeros_like(out_ref)

    h = h_ref[...]
    g = _dot(h, wg_ref[...])
    u = _dot(h, wu_ref[...])
    a = (g * jax.nn.sigmoid(g) * u).astype(BF16)
    out_ref[...] += _dot(a, wd_ref[...])

    @pl.when(j == pl.num_programs(1) - 1)
    def _():
        out_ref[...] = _rms(out_ref[...], gpost_ref[...])


def _ffn(h, wg, wu, wd, g_post, *, bm, bf):
    m, d = h.shape
    f = wg.shape[1]
    vmem = 2 * bm * d * 2 + 2 * bm * d * 4 + 2 * 3 * d * bf * 2 + 3 * bm * bf * 4
    return pl.pallas_call(
        _ffn_kernel,
        out_shape=jax.ShapeDtypeStruct((m, d), F32),
        grid=(m // bm, f // bf),
        in_specs=[
            pl.BlockSpec((bm, d), lambda i, j: (i, 0)),
            pl.BlockSpec((d, bf), lambda i, j: (0, j)),
            pl.BlockSpec((d, bf), lambda i, j: (0, j)),
            pl.BlockSpec((bf, d), lambda i, j: (j, 0)),
            pl.BlockSpec((1, d), lambda i, j: (0, 0)),
        ],
        out_specs=pl.BlockSpec((bm, d), lambda i, j: (i, 0)),
        compiler_params=_params(("parallel", "arbitrary"), vmem + (8 << 20)),
        name="ffn",
    )(h, wg, wu, wd, g_post)


def _ple_kernel(x_ref, f_ref, p_ref, gpre_ref, wpg_ref, wpp_ref, gpost_ref, out_ref):
    x = x_ref[...] + f_ref[...]
    h = _rms(x, gpre_ref[...]).astype(BF16)
    e = jax.nn.sigmoid(_dot(h, wpg_ref[...])) * _dot(p_ref[...].astype(BF16),
                                                      wpp_ref[...].astype(BF16))
    out_ref[...] = x + _rms(e, gpost_ref[...])


def _ple(x2, f2, p2, g_pre, wpg, wpp, g_post, *, bm):
    m, d = x2.shape
    pd = p2.shape[1]
    vmem = 6 * bm * d * 4 + 2 * bm * pd * 4 + d * d * 2 + pd * d * 4 + 4 * bm * d * 4
    return pl.pallas_call(
        _ple_kernel,
        out_shape=jax.ShapeDtypeStruct((m, d), F32),
        grid=(m // bm,),
        in_specs=[
            pl.BlockSpec((bm, d), lambda i: (i, 0)),
            pl.BlockSpec((bm, d), lambda i: (i, 0)),
            pl.BlockSpec((bm, pd), lambda i: (i, 0)),
            _resident(g_pre.shape),
            _resident(wpg.shape),
            _resident(wpp.shape),
            _resident(g_post.shape),
        ],
        out_specs=pl.BlockSpec((bm, d), lambda i: (i, 0)),
        compiler_params=_params(("parallel",), vmem + (8 << 20)),
        name="ple",
    )(x2, f2, p2, g_pre, wpg, wpp, g_post)


def _proj_layout(mix_a, dk_total, mix_b, d):
    names = ["a_x", "a_b", "a_c", "q", "k", "v", "og", "gate_a", "gate_b"]
    widths = [mix_a, mix_a, mix_a, dk_total, dk_total, mix_b, mix_b, d, d]
    cols, off = {}, 0
    for nme, wdt in zip(names, widths):
        assert off % wdt == 0, (nme, off, wdt)
        cols[nme] = off
        off += wdt
    return cols, cols["gate_a"]


def kernel(x, p, w_in, conv_w, w_a_out, w_alpha_up, b_alpha_up, gla_head_gain, w_b_out, w_mix_out, g_pre_mix, g_post_mix, g_pre_ffn, g_post_ffn, w_ff_gate, w_ff_up, w_ff_down, g_pre_ple, g_post_ple, w_ple_gate, w_ple_proj):
    batch, seq, d = x.shape
    depth = w_in.shape[0]
    mix_a = conv_w.shape[-1]
    rank, dk_total = w_alpha_up.shape[1], w_alpha_up.shape[2]
    dv = gla_head_gain.shape[-1]
    mix_b = w_b_out.shape[1]
    heads = mix_b // dv
    dk = dk_total // heads
    assert rank <= LANES and seq % CHUNK == 0
    cols, lr0 = _proj_layout(mix_a, dk_total, mix_b, d)

    xs = x.reshape(batch * seq, d)
    for i in range(depth):
        proj, a_lr = _norm_proj(xs, g_pre_mix[i][None], w_in[i].T.astype(BF16), lr0=lr0, rank=rank,
                                bm=1024, bn=1024)
        o, wa, wb, wm = _gla(
            proj, a_lr, w_alpha_up[i], b_alpha_up[i][None], gla_head_gain[i][None],
            [w_a_out[i], w_b_out[i], w_mix_out[i]],
            batch=batch, seq=seq, heads=heads, dk=dk, dv=dv, cols=cols, ts=256)
        xs, h_ffn, wg, wu, wd, wpg = _mix_out(
            proj, o, xs, conv_w[i], wa, wb, wm, g_post_mix[i][None], g_pre_ffn[i][None],
            [w_ff_gate[i], w_ff_up[i], w_ff_down[i], w_ple_gate[i]], seq=seq, cols=cols, bm=256)
        f_norm = _ffn(h_ffn, wg, wu, wd, g_post_ffn[i][None], bm=1024, bf=512)
        xs = _ple(xs, f_norm, p[i].reshape(batch * seq, -1), g_pre_ple[i][None], wpg,
                  w_ple_proj[i], g_post_ple[i][None], bm=512)
    return xs.reshape(batch, seq, d)
```

```python
import functools

import jax
import jax.numpy as jnp
from jax import lax
from jax.experimental import pallas as pl
from jax.experimental.pallas import tpu as pltpu

EPS = 1e-6
CHUNK = 64
GLA_TAU = 16.0
CONV_WIDTH = 3
LANES = 128
SUBLANES = 8
VMEM_LIMIT_CAP = 60000 * 1024

F32 = jnp.float32
BF16 = jnp.bfloat16


def _rms(x, gain):
    ms = jnp.mean(x * x, axis=-1, keepdims=True)
    return x * lax.rsqrt(ms + EPS) * gain


def _dot(a, b):
    return jnp.dot(a, b, preferred_element_type=F32)


def _dot_nt(a, b):
    return lax.dot_general(a, b, (((1,), (1,)), ((), ())), preferred_element_type=F32)


def _params(semantics, vmem_bytes):
    return pltpu.CompilerParams(
        dimension_semantics=semantics,
        vmem_limit_bytes=min(int(vmem_bytes), VMEM_LIMIT_CAP),
    )


def _resident(shape):
    return pl.BlockSpec(shape, lambda *_: (0,) * len(shape), pipeline_mode=pl.Buffered(1))


BF16_ROWS = 16


def _cast_plan(weights, n_slabs, slab_of):
    in_specs, out_specs, out_shapes, vmem = [], [], [], 0
    for w in weights:
        rows, width = w.shape
        assert rows % (n_slabs * BF16_ROWS) == 0, (w.shape, n_slabs)
        slab = (rows // n_slabs, width)
        spec = pl.BlockSpec(slab, lambda *idx: (slab_of(*idx), 0))
        in_specs.append(spec)
        out_specs.append(spec)
        out_shapes.append(jax.ShapeDtypeStruct(w.shape, BF16))
        vmem += 2 * slab[0] * slab[1] * (4 + 2)
    return in_specs, out_specs, out_shapes, vmem


def _cast_slabs(src_refs, dst_refs):
    for src, dst in zip(src_refs, dst_refs):
        dst[...] = src[...].astype(dst.dtype)


def _norm_proj_kernel(x_ref, g_ref, wt_ref, wlr_ref, o_ref, lr_ref, h_ref):
    @pl.when(pl.program_id(1) == 0)
    def _():
        h = _rms(x_ref[...], g_ref[...]).astype(BF16)
        h_ref[...] = h
        wlr = wlr_ref[...]
        wlr = jnp.concatenate([wlr, jnp.zeros((LANES - wlr.shape[0], wlr.shape[1]), F32)], axis=0)
        lr_ref[...] = _dot_nt(h, wlr.astype(BF16)).astype(lr_ref.dtype)

    o_ref[...] = _dot_nt(h_ref[...], wt_ref[...].astype(BF16)).astype(o_ref.dtype)


def _norm_proj(x2, gain, wt, *, lr0, rank, bm, bn):
    m, d = x2.shape
    n = wt.shape[0] - rank
    assert lr0 % bn == 0 and n % bn == 0 and rank % SUBLANES == 0

    def w_rows(i, j):
        skip = jnp.where(j * bn >= lr0, rank // SUBLANES, 0)
        return ((j * (bn // SUBLANES) + skip) * SUBLANES, 0)

    vmem = 2 * bm * d * 4 + 2 * bn * d * 4 + 2 * bm * bn * 2 + bm * d * 2 + bm * bn * 4 + d * bn * 2
    return pl.pallas_call(
        _norm_proj_kernel,
        out_shape=(jax.ShapeDtypeStruct((m, n), BF16), jax.ShapeDtypeStruct((m, LANES), BF16)),
        grid=(m // bm, n // bn),
        in_specs=[
            pl.BlockSpec((bm, d), lambda i, j: (i, 0)),
            pl.BlockSpec((1, d), lambda i, j: (0, 0)),
            pl.BlockSpec((pl.Element(bn), pl.Element(d)), w_rows),
            pl.BlockSpec((pl.Element(rank), pl.Element(d)), lambda i, j: (lr0, 0)),
        ],
        out_specs=(pl.BlockSpec((bm, bn), lambda i, j: (i, j)),
                   pl.BlockSpec((bm, LANES), lambda i, j: (i, 0))),
        scratch_shapes=[pltpu.VMEM((bm, d), BF16)],
        compiler_params=_params(("parallel", "arbitrary"), vmem + (8 << 20)),
        name="norm_proj",
    )(x2, gain, wt, wt)


def _gla_kernel(*refs, heads, dk, dv, ts, n_cast):
    q_ref, k_ref, v_ref, og_ref, alr_ref, wup_ref, bup_ref, hg_ref = refs[:8]
    o_ref = refs[8 + n_cast]
    state_ref = refs[-1]
    _cast_slabs(refs[8:8 + n_cast], refs[9 + n_cast:9 + 2 * n_cast])

    @pl.when(pl.program_id(1) == 0)
    def _():
        state_ref[...] = jnp.zeros_like(state_ref)

    w_up = wup_ref[...]
    rank = w_up.shape[0]
    w_up = jnp.concatenate([w_up, jnp.zeros((LANES - rank, w_up.shape[1]), F32)], axis=0)
    z = _dot(alr_ref[...], w_up.astype(BF16)) + bup_ref[...]
    log_a = (jnp.minimum(z, 0.0) - jnp.log(1.0 + jnp.exp(-jnp.abs(z)))) * (1.0 / GLA_TAU)

    row = lax.broadcasted_iota(jnp.int32, (ts, ts), 0)
    col = lax.broadcasted_iota(jnp.int32, (ts, ts), 1)
    tri = ((row // CHUNK == col // CHUNK) & (col <= row)).astype(BF16)
    la_hi = log_a.astype(BF16)
    la_lo = (log_a - la_hi.astype(F32)).astype(BF16)
    b = _dot(tri, la_hi) + _dot(tri, la_lo)

    crow = lax.broadcasted_iota(jnp.int32, (CHUNK, CHUNK), 0)
    ccol = lax.broadcasted_iota(jnp.int32, (CHUNK, CHUNK), 1)
    lower = ccol <= crow

    scale = dk ** -0.5
    gain = hg_ref[...]
    states = [state_ref[h] for h in range(heads)]
    for c in range(ts // CHUNK):
        r = slice(c * CHUNK, (c + 1) * CHUNK)
        bc = b[r, :]
        mid = bc[CHUNK // 2:CHUNK // 2 + 1, :]
        last = bc[CHUNK - 1:CHUNK, :]
        e_fwd = jnp.exp(bc - mid)
        e_rev = jnp.exp(mid - bc)
        qc = q_ref[r, :].astype(F32) * scale
        kc = k_ref[r, :].astype(F32)
        qe = qc * e_fwd
        ke = kc * e_rev
        q_f = qe.astype(BF16)
        k_f = ke.astype(BF16)
        q_r = (qc * e_rev).astype(BF16)
        k_r = (kc * e_fwd).astype(BF16)
        k_l = (ke * jnp.exp(last - mid)).astype(BF16)
        q_i = (qe * jnp.exp(mid)).astype(BF16)
        decay = jnp.exp(last)
        for h in range(heads):
            ck = slice(h * dk, (h + 1) * dk)
            cv = slice(h * dv, (h + 1) * dv)
            vh = v_ref[r, cv]
            att = jnp.where(lower, _dot_nt(q_f[:, ck], k_f[:, ck]), _dot_nt(q_r[:, ck], k_r[:, ck]))
            av = _dot(jnp.concatenate([att.astype(BF16), k_l[:, ck].T], axis=0), vh)
            st = states[h]
            o = av[:CHUNK] + _dot(q_i[:, ck], st.astype(BF16))
            dcol = jnp.transpose(jnp.broadcast_to(decay[:, ck], (dk, dk)))
            states[h] = st * jnp.concatenate([dcol] * (dv // dk), axis=1) + av[CHUNK:]
            o = o * lax.rsqrt(jnp.mean(o * o, axis=-1, keepdims=True) + EPS) * gain
            og = og_ref[r, cv].astype(F32)
            o_ref[r, cv] = (o * (og * jax.nn.sigmoid(og))).astype(o_ref.dtype)
    for h in range(heads):
        state_ref[h] = states[h]


def _gla(proj, a_lr, w_up, b_up, head_gain, cast_weights, *, batch, seq, heads, dk, dv, cols, ts):
    m = proj.shape[0]
    ns = seq // ts
    dkt, dvt = heads * dk, heads * dv
    rank_pad = a_lr.shape[1]

    def rows(b, s):
        return b * ns + s

    c_in, c_out, c_shapes, c_vmem = _cast_plan(cast_weights, batch * ns, rows)
    kernel = functools.partial(_gla_kernel, heads=heads, dk=dk, dv=dv, ts=ts, n_cast=len(c_in))
    vmem = (2 * ts * (2 * dkt + 2 * dvt + rank_pad) * 2 + 2 * ts * dvt * 2 + heads * dk * dv * 4
            + c_vmem)
    return pl.pallas_call(
        kernel,
        out_shape=[jax.ShapeDtypeStruct((m, dvt), BF16)] + c_shapes,
        grid=(batch, ns),
        in_specs=[
            pl.BlockSpec((ts, dkt), lambda b, s: (rows(b, s), cols["q"] // dkt)),
            pl.BlockSpec((ts, dkt), lambda b, s: (rows(b, s), cols["k"] // dkt)),
            pl.BlockSpec((ts, dvt), lambda b, s: (rows(b, s), cols["v"] // dvt)),
            pl.BlockSpec((ts, dvt), lambda b, s: (rows(b, s), cols["og"] // dvt)),
            pl.BlockSpec((ts, rank_pad), lambda b, s: (rows(b, s), 0)),
            _resident(w_up.shape),
            _resident(b_up.shape),
            _resident(head_gain.shape),
        ] + c_in,
        out_specs=[pl.BlockSpec((ts, dvt), lambda b, s: (rows(b, s), 0))] + c_out,
        scratch_shapes=[pltpu.VMEM((heads, dk, dv), F32)],
        compiler_params=_params(("arbitrary", "arbitrary"), vmem + (24 << 20)),
        name="gla",
    )(proj, proj, proj, proj, a_lr, w_up, b_up, head_gain, *cast_weights)


def _mix_out_kernel(*refs, tiles_per_seq, n_cast):
    (ax_ref, ab_ref, ac_ref, axp_ref, acp_ref, o_ref, ga_ref, gb_ref, x_ref,
     cw_ref, wa_ref, wb_ref, wm_ref, g_ref, gnext_ref) = refs[:15]
    out_ref, hnext_ref = refs[15 + n_cast:17 + n_cast]
    _cast_slabs(refs[15:15 + n_cast], refs[17 + n_cast:])

    u = ac_ref[...].astype(F32) * ax_ref[...].astype(F32)
    u_prev = acp_ref[...].astype(F32) * axp_ref[...].astype(F32)
    first = (pl.program_id(0) % tiles_per_seq) == 0
    u_prev = jnp.where(first, 0.0, u_prev)
    p1 = u_prev[SUBLANES - 1:SUBLANES, :]
    p2 = u_prev[SUBLANES - 2:SUBLANES - 1, :]
    row = lax.broadcasted_iota(jnp.int32, u.shape, 0)
    u1 = jnp.where(row == 0, p1, pltpu.roll(u, 1, 0))
    u2 = jnp.where(row == 0, p2, jnp.where(row == 1, p1, pltpu.roll(u, 2, 0)))
    cw = cw_ref[...]
    y = cw[0:1, :] * u2 + cw[1:2, :] * u1 + cw[2:3, :] * u
    ca = (ab_ref[...].astype(F32) * y).astype(BF16)

    ya = _dot(ca, wa_ref[...])
    yb = _dot(o_ref[...], wb_ref[...])
    mix = jax.nn.sigmoid(ga_ref[...].astype(F32)) * ya + jax.nn.sigmoid(gb_ref[...].astype(F32)) * yb
    m = _dot(mix.astype(BF16), wm_ref[...])
    x_new = x_ref[...] + _rms(m, g_ref[...])
    out_ref[...] = x_new
    hnext_ref[...] = _rms(x_new, gnext_ref[...]).astype(hnext_ref.dtype)


def _mix_out(proj, o, x2, conv_w, wa, wb, wm, gain, gain_next, cast_weights, *, seq, cols, bm):
    m, d = x2.shape
    ca = conv_w.shape[1]
    cb = o.shape[1]
    tiles_per_seq = seq // bm
    halo = bm // SUBLANES

    def prev(i):
        return jnp.maximum(i * halo - 1, 0)

    c_in, c_out, c_shapes, c_vmem = _cast_plan(cast_weights, m // bm, lambda i: i)
    kernel = functools.partial(_mix_out_kernel, tiles_per_seq=tiles_per_seq, n_cast=len(c_in))
    vmem = (2 * bm * (3 * ca + cb + 3 * d) * 2 + 4 * bm * d * 4
            + (ca + cb + d) * d * 2 + 4 * bm * d * 4 + c_vmem)
    return pl.pallas_call(
        kernel,
        out_shape=[jax.ShapeDtypeStruct((m, d), F32), jax.ShapeDtypeStruct((m, d), BF16)] + c_shapes,
        grid=(m // bm,),
        in_specs=[
            pl.BlockSpec((bm, ca), lambda i: (i, cols["a_x"] // ca)),
            pl.BlockSpec((bm, ca), lambda i: (i, cols["a_b"] // ca)),
            pl.BlockSpec((bm, ca), lambda i: (i, cols["a_c"] // ca)),
            pl.BlockSpec((SUBLANES, ca), lambda i: (prev(i), cols["a_x"] // ca)),
            pl.BlockSpec((SUBLANES, ca), lambda i: (prev(i), cols["a_c"] // ca)),
            pl.BlockSpec((bm, cb), lambda i: (i, 0)),
            pl.BlockSpec((bm, d), lambda i: (i, cols["gate_a"] // d)),
            pl.BlockSpec((bm, d), lambda i: (i, cols["gate_b"] // d)),
            pl.BlockSpec((bm, d), lambda i: (i, 0)),
            _resident(conv_w.shape),
            _resident(wa.shape),
            _resident(wb.shape),
            _resident(wm.shape),
            _resident(gain.shape),
            _resident(gain_next.shape),
        ] + c_in,
        out_specs=[pl.BlockSpec((bm, d), lambda i: (i, 0))] * 2 + c_out,
        compiler_params=_params(("arbitrary",), vmem + (8 << 20)),
        name="mix_out",
    )(proj, proj, proj, proj, proj, o, proj, proj, x2, conv_w, wa, wb, wm, gain, gain_next,
      *cast_weights)


def _ffn_kernel(h_ref, wg_ref, wu_ref, wd_ref, gpost_ref, out_ref):
    j = pl.program_id(1)

    @pl.when(j == 0)
    def _():
        out_ref[...] = jnp.zeros_like(out_ref)

    h = h_ref[...]
    g = _dot(h, wg_ref[...])
    u = _dot(h, wu_ref[...])
    a = (g * jax.nn.sigmoid(g) * u).astype(BF16)
    out_ref[...] += _dot(a, wd_ref[...])

    @pl.when(j == pl.num_programs(1) - 1)
    def _():
        out_ref[...] = _rms(out_ref[...], gpost_ref[...])


def _ffn(h, wg, wu, wd, g_post, *, bm, bf):
    m, d = h.shape
    f = wg.shape[1]
    vmem = 2 * bm * d * 2 + 2 * bm * d * 4 + 2 * 3 * d * bf * 2 + 3 * bm * bf * 4
    return pl.pallas_call(
        _ffn_kernel,
        out_shape=jax.ShapeDtypeStruct((m, d), F32),
        grid=(m // bm, f // bf),
        in_specs=[
            pl.BlockSpec((bm, d), lambda i, j: (i, 0)),
            pl.BlockSpec((d, bf), lambda i, j: (0, j)),
            pl.BlockSpec((d, bf), lambda i, j: (0, j)),
            pl.BlockSpec((bf, d), lambda i, j: (j, 0)),
            pl.BlockSpec((1, d), lambda i, j: (0, 0)),
        ],
        out_specs=pl.BlockSpec((bm, d), lambda i, j: (i, 0)),
        compiler_params=_params(("parallel", "arbitrary"), vmem + (8 << 20)),
        name="ffn",
    )(h, wg, wu, wd, g_post)


def _ple_kernel(x_ref, f_ref, p_ref, gpre_ref, wpg_ref, wpp_ref, gpost_ref, out_ref):
    x = x_ref[...] + f_ref[...]
    h = _rms(x, gpre_ref[...]).astype(BF16)
    e = jax.nn.sigmoid(_dot(h, wpg_ref[...])) * _dot(p_ref[...].astype(BF16),
                                                      wpp_ref[...].astype(BF16))
    out_ref[...] = x + _rms(e, gpost_ref[...])


def _ple(x2, f2, p2, g_pre, wpg, wpp, g_post, *, bm):
    m, d = x2.shape
    pd = p2.shape[1]
    vmem = 6 * bm * d * 4 + 2 * bm * pd * 4 + d * d * 2 + pd * d * 4 + 4 * bm * d * 4
    return pl.pallas_call(
        _ple_kernel,
        out_shape=jax.ShapeDtypeStruct((m, d), F32),
        grid=(m // bm,),
        in_specs=[
            pl.BlockSpec((bm, d), lambda i: (i, 0)),
            pl.BlockSpec((bm, d), lambda i: (i, 0)),
            pl.BlockSpec((bm, pd), lambda i: (i, 0)),
            _resident(g_pre.shape),
            _resident(wpg.shape),
            _resident(wpp.shape),
            _resident(g_post.shape),
        ],
        out_specs=pl.BlockSpec((bm, d), lambda i: (i, 0)),
        compiler_params=_params(("parallel",), vmem + (8 << 20)),
        name="ple",
    )(x2, f2, p2, g_pre, wpg, wpp, g_post)


def _proj_layout(mix_a, dk_total, mix_b, d):
    names = ["a_x", "a_b", "a_c", "q", "k", "v", "og", "gate_a", "gate_b"]
    widths = [mix_a, mix_a, mix_a, dk_total, dk_total, mix_b, mix_b, d, d]
    cols, off = {}, 0
    for nme, wdt in zip(names, widths):
        assert off % wdt == 0, (nme, off, wdt)
        cols[nme] = off
        off += wdt
    return cols, cols["gate_a"]


def kernel(x, p, w_in, conv_w, w_a_out, w_alpha_up, b_alpha_up, gla_head_gain, w_b_out, w_mix_out, g_pre_mix, g_post_mix, g_pre_ffn, g_post_ffn, w_ff_gate, w_ff_up, w_ff_down, g_pre_ple, g_post_ple, w_ple_gate, w_ple_proj):
    batch, seq, d = x.shape
    depth = w_in.shape[0]
    mix_a = conv_w.shape[-1]
    rank, dk_total = w_alpha_up.shape[1], w_alpha_up.shape[2]
    dv = gla_head_gain.shape[-1]
    mix_b = w_b_out.shape[1]
    heads = mix_b // dv
    dk = dk_total // heads
    assert rank <= LANES and seq % CHUNK == 0
    cols, lr0 = _proj_layout(mix_a, dk_total, mix_b, d)

    xs = x.reshape(batch * seq, d)
    for i in range(depth):
        proj, a_lr = _norm_proj(xs, g_pre_mix[i][None], w_in[i].T, lr0=lr0, rank=rank,
                                bm=1024, bn=1024)
        o, wa, wb, wm = _gla(
            proj, a_lr, w_alpha_up[i], b_alpha_up[i][None], gla_head_gain[i][None],
            [w_a_out[i], w_b_out[i], w_mix_out[i]],
            batch=batch, seq=seq, heads=heads, dk=dk, dv=dv, cols=cols, ts=512)
        xs, h_ffn, wg, wu, wd, wpg = _mix_out(
            proj, o, xs, conv_w[i], wa, wb, wm, g_post_mix[i][None], g_pre_ffn[i][None],
            [w_ff_gate[i], w_ff_up[i], w_ff_down[i], w_ple_gate[i]], seq=seq, cols=cols, bm=256)
        f_norm = _ffn(h_ffn, wg, wu, wd, g_post_ffn[i][None], bm=1024, bf=512)
        xs = _ple(xs, f_norm, p[i].reshape(batch * seq, -1), g_pre_ple[i][None], wpg,
                  w_ple_proj[i], g_post_ple[i][None], bm=512)
    return xs.reshape(batch, seq, d)
```

```python
import functools

import jax
import jax.numpy as jnp
from jax import lax
from jax.experimental import pallas as pl
from jax.experimental.pallas import tpu as pltpu

EPS = 1e-6
CHUNK = 64
GLA_TAU = 16.0
CONV_WIDTH = 3
LANES = 128
SUBLANES = 8
VMEM_LIMIT_CAP = 60000 * 1024

F32 = jnp.float32
BF16 = jnp.bfloat16


def _rms(x, gain):
    ms = jnp.mean(x * x, axis=-1, keepdims=True)
    return x * lax.rsqrt(ms + EPS) * gain


def _dot(a, b):
    return jnp.dot(a, b, preferred_element_type=F32)


def _dot_nt(a, b):
    return lax.dot_general(a, b, (((1,), (1,)), ((), ())), preferred_element_type=F32)


def _params(semantics, vmem_bytes):
    return pltpu.CompilerParams(
        dimension_semantics=semantics,
        vmem_limit_bytes=min(int(vmem_bytes), VMEM_LIMIT_CAP),
    )


def _resident(shape):
    return pl.BlockSpec(shape, lambda *_: (0,) * len(shape), pipeline_mode=pl.Buffered(1))


BF16_ROWS = 16


def _cast_plan(weights, n_slabs, slab_of):
    in_specs, out_specs, out_shapes, vmem = [], [], [], 0
    for w in weights:
        rows, width = w.shape
        assert rows % (n_slabs * BF16_ROWS) == 0, (w.shape, n_slabs)
        slab = (rows // n_slabs, width)
        spec = pl.BlockSpec(slab, lambda *idx: (slab_of(*idx), 0))
        in_specs.append(spec)
        out_specs.append(spec)
        out_shapes.append(jax.ShapeDtypeStruct(w.shape, BF16))
        vmem += 2 * slab[0] * slab[1] * (4 + 2)
    return in_specs, out_specs, out_shapes, vmem


def _cast_slabs(src_refs, dst_refs):
    for src, dst in zip(src_refs, dst_refs):
        dst[...] = src[...].astype(dst.dtype)


X_PARTS = 4


def _norm_proj_kernel(*refs):
    x_refs = refs[:X_PARTS]
    g_ref, wt_ref, wlr_ref, o_ref, lr_ref, h_ref = refs[X_PARTS:]

    @pl.when(pl.program_id(1) == 0)
    def _():
        d = h_ref.shape[1]
        w = d // X_PARTS
        ms = sum(jnp.sum(r[...] * r[...], axis=-1, keepdims=True) for r in x_refs) * (1.0 / d)
        rs = lax.rsqrt(ms + EPS)
        for k, r in enumerate(x_refs):
            c = slice(k * w, (k + 1) * w)
            h_ref[:, c] = (r[...] * rs * g_ref[:, c]).astype(BF16)
        wlr = wlr_ref[...]
        wlr = jnp.concatenate([wlr, jnp.zeros((LANES - wlr.shape[0], wlr.shape[1]), F32)], axis=0)
        lr_ref[...] = _dot_nt(h_ref[...], wlr.astype(BF16)).astype(lr_ref.dtype)

    o_ref[...] = _dot_nt(h_ref[...], wt_ref[...].astype(BF16)).astype(o_ref.dtype)


def _norm_proj(x2, gain, wt, *, lr0, rank, bm, bn):
    m, d = x2.shape
    n = wt.shape[0] - rank
    assert lr0 % bn == 0 and n % bn == 0 and rank % SUBLANES == 0

    def w_rows(i, j):
        skip = jnp.where(j * bn >= lr0, rank // SUBLANES, 0)
        return ((j * (bn // SUBLANES) + skip) * SUBLANES, 0)

    ni, nj = m // bm, n // bn
    switch = [(k + 1) * nj // (X_PARTS + 1) for k in range(X_PARTS)]
    assert d % (X_PARTS * LANES) == 0 and all(0 < s < nj for s in switch)

    def x_part(k):
        def index(i, j):
            return (jnp.minimum(i + jnp.where(j >= switch[k], 1, 0), ni - 1), k)
        return pl.BlockSpec((bm, d // X_PARTS), index)

    vmem = 2 * bm * d * 4 + 2 * bn * d * 4 + 2 * bm * bn * 2 + bm * d * 2 + bm * bn * 4 + d * bn * 2
    return pl.pallas_call(
        _norm_proj_kernel,
        out_shape=(jax.ShapeDtypeStruct((m, n), BF16), jax.ShapeDtypeStruct((m, LANES), BF16)),
        grid=(ni, nj),
        in_specs=[x_part(k) for k in range(X_PARTS)] + [
            pl.BlockSpec((1, d), lambda i, j: (0, 0)),
            pl.BlockSpec((pl.Element(bn), pl.Element(d)), w_rows),
            pl.BlockSpec((pl.Element(rank), pl.Element(d)), lambda i, j: (lr0, 0)),
        ],
        out_specs=(pl.BlockSpec((bm, bn), lambda i, j: (i, j)),
                   pl.BlockSpec((bm, LANES), lambda i, j: (i, 0))),
        scratch_shapes=[pltpu.VMEM((bm, d), BF16)],
        compiler_params=_params(("arbitrary", "arbitrary"), vmem + (8 << 20)),
        name="norm_proj",
    )(*([x2] * X_PARTS), gain, wt, wt)


def _gla_kernel(*refs, heads, dk, dv, ts, n_cast):
    q_ref, k_ref, v_ref, og_ref, alr_ref, wup_ref, bup_ref, hg_ref = refs[:8]
    o_ref = refs[8 + n_cast]
    state_ref = refs[-1]
    _cast_slabs(refs[8:8 + n_cast], refs[9 + n_cast:9 + 2 * n_cast])

    @pl.when(pl.program_id(1) == 0)
    def _():
        state_ref[...] = jnp.zeros_like(state_ref)

    w_up = wup_ref[...]
    rank = w_up.shape[0]
    w_up = jnp.concatenate([w_up, jnp.zeros((LANES - rank, w_up.shape[1]), F32)], axis=0)
    z = _dot(alr_ref[...], w_up.astype(BF16)) + bup_ref[...]
    log_a = (jnp.minimum(z, 0.0) - jnp.log(1.0 + jnp.exp(-jnp.abs(z)))) * (1.0 / GLA_TAU)

    row = lax.broadcasted_iota(jnp.int32, (ts, ts), 0)
    col = lax.broadcasted_iota(jnp.int32, (ts, ts), 1)
    tri = ((row // CHUNK == col // CHUNK) & (col <= row)).astype(BF16)
    la_hi = log_a.astype(BF16)
    la_lo = (log_a - la_hi.astype(F32)).astype(BF16)
    b = _dot(tri, la_hi) + _dot(tri, la_lo)

    crow = lax.broadcasted_iota(jnp.int32, (CHUNK, CHUNK), 0)
    ccol = lax.broadcasted_iota(jnp.int32, (CHUNK, CHUNK), 1)
    lower = ccol <= crow

    scale = dk ** -0.5
    gain = hg_ref[...]
    states = [state_ref[h] for h in range(heads)]
    for c in range(ts // CHUNK):
        r = slice(c * CHUNK, (c + 1) * CHUNK)
        bc = b[r, :]
        mid = bc[CHUNK // 2:CHUNK // 2 + 1, :]
        last = bc[CHUNK - 1:CHUNK, :]
        e_fwd = jnp.exp(bc - mid)
        e_rev = jnp.exp(mid - bc)
        qc = q_ref[r, :].astype(F32) * scale
        kc = k_ref[r, :].astype(F32)
        qe = qc * e_fwd
        ke = kc * e_rev
        q_f = qe.astype(BF16)
        k_f = ke.astype(BF16)
        q_r = (qc * e_rev).astype(BF16)
        k_r = (kc * e_fwd).astype(BF16)
        k_l = (ke * jnp.exp(last - mid)).astype(BF16)
        q_i = (qe * jnp.exp(mid)).astype(BF16)
        decay = jnp.exp(last)
        for h in range(heads):
            ck = slice(h * dk, (h + 1) * dk)
            cv = slice(h * dv, (h + 1) * dv)
            vh = v_ref[r, cv]
            att = jnp.where(lower, _dot_nt(q_f[:, ck], k_f[:, ck]), _dot_nt(q_r[:, ck], k_r[:, ck]))
            av = _dot(jnp.concatenate([att.astype(BF16), k_l[:, ck].T], axis=0), vh)
            st = states[h]
            o = av[:CHUNK] + _dot(q_i[:, ck], st.astype(BF16))
            dcol = jnp.transpose(jnp.broadcast_to(decay[:, ck], (dk, dk)))
            states[h] = st * jnp.concatenate([dcol] * (dv // dk), axis=1) + av[CHUNK:]
            o = o * lax.rsqrt(jnp.mean(o * o, axis=-1, keepdims=True) + EPS) * gain
            og = og_ref[r, cv].astype(F32)
            o_ref[r, cv] = (o * (og * jax.nn.sigmoid(og))).astype(o_ref.dtype)
    for h in range(heads):
        state_ref[h] = states[h]


def _gla(proj, a_lr, w_up, b_up, head_gain, cast_weights, *, batch, seq, heads, dk, dv, cols, ts):
    m = proj.shape[0]
    ns = seq // ts
    dkt, dvt = heads * dk, heads * dv
    rank_pad = a_lr.shape[1]

    def rows(b, s):
        return b * ns + s

    c_in, c_out, c_shapes, c_vmem = _cast_plan(cast_weights, batch * ns, rows)
    kernel = functools.partial(_gla_kernel, heads=heads, dk=dk, dv=dv, ts=ts, n_cast=len(c_in))
    vmem = (2 * ts * (2 * dkt + 2 * dvt + rank_pad) * 2 + 2 * ts * dvt * 2 + heads * dk * dv * 4
            + c_vmem)
    return pl.pallas_call(
        kernel,
        out_shape=[jax.ShapeDtypeStruct((m, dvt), BF16)] + c_shapes,
        grid=(batch, ns),
        in_specs=[
            pl.BlockSpec((ts, dkt), lambda b, s: (rows(b, s), cols["q"] // dkt)),
            pl.BlockSpec((ts, dkt), lambda b, s: (rows(b, s), cols["k"] // dkt)),
            pl.BlockSpec((ts, dvt), lambda b, s: (rows(b, s), cols["v"] // dvt)),
            pl.BlockSpec((ts, dvt), lambda b, s: (rows(b, s), cols["og"] // dvt)),
            pl.BlockSpec((ts, rank_pad), lambda b, s: (rows(b, s), 0)),
            _resident(w_up.shape),
            _resident(b_up.shape),
            _resident(head_gain.shape),
        ] + c_in,
        out_specs=[pl.BlockSpec((ts, dvt), lambda b, s: (rows(b, s), 0))] + c_out,
        scratch_shapes=[pltpu.VMEM((heads, dk, dv), F32)],
        compiler_params=_params(("arbitrary", "arbitrary"), vmem + (24 << 20)),
        name="gla",
    )(proj, proj, proj, proj, a_lr, w_up, b_up, head_gain, *cast_weights)


def _mix_out_kernel(*refs, tiles_per_seq, n_cast):
    (ax_ref, ab_ref, ac_ref, axp_ref, acp_ref, o_ref, ga_ref, gb_ref, x_ref,
     cw_ref, wa_ref, wb_ref, wm_ref, g_ref, gnext_ref) = refs[:15]
    out_ref, hnext_ref = refs[15 + n_cast:17 + n_cast]
    _cast_slabs(refs[15:15 + n_cast], refs[17 + n_cast:])

    u = ac_ref[...].astype(F32) * ax_ref[...].astype(F32)
    u_prev = acp_ref[...].astype(F32) * axp_ref[...].astype(F32)
    first = (pl.program_id(0) % tiles_per_seq) == 0
    u_prev = jnp.where(first, 0.0, u_prev)
    p1 = u_prev[SUBLANES - 1:SUBLANES, :]
    p2 = u_prev[SUBLANES - 2:SUBLANES - 1, :]
    row = lax.broadcasted_iota(jnp.int32, u.shape, 0)
    u1 = jnp.where(row == 0, p1, pltpu.roll(u, 1, 0))
    u2 = jnp.where(row == 0, p2, jnp.where(row == 1, p1, pltpu.roll(u, 2, 0)))
    cw = cw_ref[...]
    y = cw[0:1, :] * u2 + cw[1:2, :] * u1 + cw[2:3, :] * u
    ca = (ab_ref[...].astype(F32) * y).astype(BF16)

    ya = _dot(ca, wa_ref[...])
    yb = _dot(o_ref[...], wb_ref[...])
    mix = jax.nn.sigmoid(ga_ref[...].astype(F32)) * ya + jax.nn.sigmoid(gb_ref[...].astype(F32)) * yb
    m = _dot(mix.astype(BF16), wm_ref[...])
    x_new = x_ref[...] + _rms(m, g_ref[...])
    out_ref[...] = x_new
    hnext_ref[...] = _rms(x_new, gnext_ref[...]).astype(hnext_ref.dtype)


def _mix_out(proj, o, x2, conv_w, wa, wb, wm, gain, gain_next, cast_weights, *, seq, cols, bm):
    m, d = x2.shape
    ca = conv_w.shape[1]
    cb = o.shape[1]
    tiles_per_seq = seq // bm
    halo = bm // SUBLANES

    def prev(i):
        return jnp.maximum(i * halo - 1, 0)

    c_in, c_out, c_shapes, c_vmem = _cast_plan(cast_weights, m // bm, lambda i: i)
    kernel = functools.partial(_mix_out_kernel, tiles_per_seq=tiles_per_seq, n_cast=len(c_in))
    vmem = (2 * bm * (3 * ca + cb + 3 * d) * 2 + 4 * bm * d * 4
            + (ca + cb + d) * d * 2 + 4 * bm * d * 4 + c_vmem)
    return pl.pallas_call(
        kernel,
        out_shape=[jax.ShapeDtypeStruct((m, d), F32), jax.ShapeDtypeStruct((m, d), BF16)] + c_shapes,
        grid=(m // bm,),
        in_specs=[
            pl.BlockSpec((bm, ca), lambda i: (i, cols["a_x"] // ca)),
            pl.BlockSpec((bm, ca), lambda i: (i, cols["a_b"] // ca)),
            pl.BlockSpec((bm, ca), lambda i: (i, cols["a_c"] // ca)),
            pl.BlockSpec((SUBLANES, ca), lambda i: (prev(i), cols["a_x"] // ca)),
            pl.BlockSpec((SUBLANES, ca), lambda i: (prev(i), cols["a_c"] // ca)),
            pl.BlockSpec((bm, cb), lambda i: (i, 0)),
            pl.BlockSpec((bm, d), lambda i: (i, cols["gate_a"] // d)),
            pl.BlockSpec((bm, d), lambda i: (i, cols["gate_b"] // d)),
            pl.BlockSpec((bm, d), lambda i: (i, 0)),
            _resident(conv_w.shape),
            _resident(wa.shape),
            _resident(wb.shape),
            _resident(wm.shape),
            _resident(gain.shape),
            _resident(gain_next.shape),
        ] + c_in,
        out_specs=[pl.BlockSpec((bm, d), lambda i: (i, 0))] * 2 + c_out,
        compiler_params=_params(("arbitrary",), vmem + (8 << 20)),
        name="mix_out",
    )(proj, proj, proj, proj, proj, o, proj, proj, x2, conv_w, wa, wb, wm, gain, gain_next,
      *cast_weights)


def _ffn_kernel(h_ref, wg_ref, wu_ref, wd_ref, gpost_ref, out_ref):
    j = pl.program_id(1)

    @pl.when(j == 0)
    def _():
        out_ref[...] = jnp.zeros_like(out_ref)

    h = h_ref[...]
    g = _dot(h, wg_ref[...])
    u = _dot(h, wu_ref[...])
    a = (g * jax.nn.sigmoid(g) * u).astype(BF16)
    out_ref[...] += _dot(a, wd_ref[...])

    @pl.when(j == pl.num_programs(1) - 1)
    def _():
        out_ref[...] = _rms(out_ref[...], gpost_ref[...])


def _ffn(h, wg, wu, wd, g_post, *, bm, bf):
    m, d = h.shape
    f = wg.shape[1]
    vmem = 2 * bm * d * 2 + 2 * bm * d * 4 + 2 * 3 * d * bf * 2 + 3 * bm * bf * 4
    return pl.pallas_call(
        _ffn_kernel,
        out_shape=jax.ShapeDtypeStruct((m, d), F32),
        grid=(m // bm, f // bf),
        in_specs=[
            pl.BlockSpec((bm, d), lambda i, j: (i, 0)),
            pl.BlockSpec((d, bf), lambda i, j: (0, j)),
            pl.BlockSpec((d, bf), lambda i, j: (0, j)),
            pl.BlockSpec((bf, d), lambda i, j: (j, 0)),
            pl.BlockSpec((1, d), lambda i, j: (0, 0)),
        ],
        out_specs=pl.BlockSpec((bm, d), lambda i, j: (i, 0)),
        compiler_params=_params(("parallel", "arbitrary"), vmem + (8 << 20)),
        name="ffn",
    )(h, wg, wu, wd, g_post)


def _ple_kernel(x_ref, f_ref, p_ref, gpre_ref, wpg_ref, wpp_ref, gpost_ref, out_ref):
    x = x_ref[...] + f_ref[...]
    h = _rms(x, gpre_ref[...]).astype(BF16)
    e = jax.nn.sigmoid(_dot(h, wpg_ref[...])) * _dot(p_ref[...].astype(BF16),
                                                      wpp_ref[...].astype(BF16))
    out_ref[...] = x + _rms(e, gpost_ref[...])


def _ple(x2, f2, p2, g_pre, wpg, wpp, g_post, *, bm):
    m, d = x2.shape
    pd = p2.shape[1]
    vmem = 6 * bm * d * 4 + 2 * bm * pd * 4 + d * d * 2 + pd * d * 4 + 4 * bm * d * 4
    return pl.pallas_call(
        _ple_kernel,
        out_shape=jax.ShapeDtypeStruct((m, d), F32),
        grid=(m // bm,),
        in_specs=[
            pl.BlockSpec((bm, d), lambda i: (i, 0)),
            pl.BlockSpec((bm, d), lambda i: (i, 0)),
            pl.BlockSpec((bm, pd), lambda i: (i, 0)),
            _resident(g_pre.shape),
            _resident(wpg.shape),
            _resident(wpp.shape),
            _resident(g_post.shape),
        ],
        out_specs=pl.BlockSpec((bm, d), lambda i: (i, 0)),
        compiler_params=_params(("parallel",), vmem + (8 << 20)),
        name="ple",
    )(x2, f2, p2, g_pre, wpg, wpp, g_post)


def _proj_layout(mix_a, dk_total, mix_b, d):
    names = ["a_x", "a_b", "a_c", "q", "k", "v", "og", "gate_a", "gate_b"]
    widths = [mix_a, mix_a, mix_a, dk_total, dk_total, mix_b, mix_b, d, d]
    cols, off = {}, 0
    for nme, wdt in zip(names, widths):
        assert off % wdt == 0, (nme, off, wdt)
        cols[nme] = off
        off += wdt
    return cols, cols["gate_a"]


def kernel(x, p, w_in, conv_w, w_a_out, w_alpha_up, b_alpha_up, gla_head_gain, w_b_out, w_mix_out, g_pre_mix, g_post_mix, g_pre_ffn, g_post_ffn, w_ff_gate, w_ff_up, w_ff_down, g_pre_ple, g_post_ple, w_ple_gate, w_ple_proj):
    batch, seq, d = x.shape
    depth = w_in.shape[0]
    mix_a = conv_w.shape[-1]
    rank, dk_total = w_alpha_up.shape[1], w_alpha_up.shape[2]
    dv = gla_head_gain.shape[-1]
    mix_b = w_b_out.shape[1]
    heads = mix_b // dv
    dk = dk_total // heads
    assert rank <= LANES and seq % CHUNK == 0
    cols, lr0 = _proj_layout(mix_a, dk_total, mix_b, d)

    xs = x.reshape(batch * seq, d)
    for i in range(depth):
        proj, a_lr = _norm_proj(xs, g_pre_mix[i][None], w_in[i].T, lr0=lr0, rank=rank,
                                bm=1024, bn=1024)
        o, wa, wb, wm = _gla(
            proj, a_lr, w_alpha_up[i], b_alpha_up[i][None], gla_head_gain[i][None],
            [w_a_out[i], w_b_out[i], w_mix_out[i]],
            batch=batch, seq=seq, heads=heads, dk=dk, dv=dv, cols=cols, ts=256)
        xs, h_ffn, wg, wu, wd, wpg = _mix_out(
            proj, o, xs, conv_w[i], wa, wb, wm, g_post_mix[i][None], g_pre_ffn[i][None],
            [w_ff_gate[i], w_ff_up[i], w_ff_down[i], w_ple_gate[i]], seq=seq, cols=cols, bm=256)
        f_norm = _ffn(h_ffn, wg, wu, wd, g_post_ffn[i][None], bm=1024, bf=512)
        xs = _ple(xs, f_norm, p[i].reshape(batch * seq, -1), g_pre_ple[i][None], wpg,
                  w_ple_proj[i], g_post_ple[i][None], bm=512)
    return xs.reshape(batch, seq, d)
```

```python
import functools

import jax
import jax.numpy as jnp
from jax import lax
from jax.experimental import pallas as pl
from jax.experimental.pallas import tpu as pltpu

EPS = 1e-6
CHUNK = 64
GLA_TAU = 16.0
CONV_WIDTH = 3
LANES = 128
SUBLANES = 8
VMEM_LIMIT_CAP = 60000 * 1024

F32 = jnp.float32
BF16 = jnp.bfloat16


def _rms(x, gain):
    ms = jnp.mean(x * x, axis=-1, keepdims=True)
    return x * lax.rsqrt(ms + EPS) * gain


def _dot(a, b):
    return jnp.dot(a, b, preferred_element_type=F32)


def _dot_nt(a, b):
    return lax.dot_general(a, b, (((1,), (1,)), ((), ())), preferred_element_type=F32)


def _params(semantics, vmem_bytes):
    return pltpu.CompilerParams(
        dimension_semantics=semantics,
        vmem_limit_bytes=min(int(vmem_bytes), VMEM_LIMIT_CAP),
    )


def _resident(shape):
    return pl.BlockSpec(shape, lambda *_: (0,) * len(shape), pipeline_mode=pl.Buffered(1))


BF16_ROWS = 16


def _cast_plan(weights, n_slabs, slab_of):
    in_specs, out_specs, out_shapes, vmem = [], [], [], 0
    for w in weights:
        rows, width = w.shape
        assert rows % (n_slabs * BF16_ROWS) == 0, (w.shape, n_slabs)
        slab = (rows // n_slabs, width)
        spec = pl.BlockSpec(slab, lambda *idx: (slab_of(*idx), 0))
        in_specs.append(spec)
        out_specs.append(spec)
        out_shapes.append(jax.ShapeDtypeStruct(w.shape, BF16))
        vmem += 2 * slab[0] * slab[1] * (4 + 2)
    return in_specs, out_specs, out_shapes, vmem


def _cast_slabs(src_refs, dst_refs):
    for src, dst in zip(src_refs, dst_refs):
        dst[...] = src[...].astype(dst.dtype)


X_PARTS = 4


def _norm_proj_kernel(*refs):
    x_refs = refs[:X_PARTS]
    g_ref, wt_ref, wlr_ref, o_ref, lr_ref, h_ref = refs[X_PARTS:]

    @pl.when(pl.program_id(1) == 0)
    def _():
        d = h_ref.shape[1]
        w = d // X_PARTS
        ms = sum(jnp.sum(r[...] * r[...], axis=-1, keepdims=True) for r in x_refs) * (1.0 / d)
        rs = lax.rsqrt(ms + EPS)
        for k, r in enumerate(x_refs):
            c = slice(k * w, (k + 1) * w)
            h_ref[:, c] = (r[...] * rs * g_ref[:, c]).astype(BF16)
        wlr = wlr_ref[...]
        wlr = jnp.concatenate([wlr, jnp.zeros((LANES - wlr.shape[0], wlr.shape[1]), F32)], axis=0)
        lr_ref[...] = _dot_nt(h_ref[...], wlr.astype(BF16)).astype(lr_ref.dtype)

    o_ref[...] = _dot_nt(h_ref[...], wt_ref[...].astype(BF16)).astype(o_ref.dtype)


def _norm_proj(x2, gain, wt, *, lr0, rank, bm, bn):
    m, d = x2.shape
    n = wt.shape[0] - rank
    assert lr0 % bn == 0 and n % bn == 0 and rank % SUBLANES == 0

    def w_rows(i, j):
        skip = jnp.where(j * bn >= lr0, rank // SUBLANES, 0)
        return ((j * (bn // SUBLANES) + skip) * SUBLANES, 0)

    ni, nj = m // bm, n // bn
    switch = [(k + 1) * nj // (X_PARTS + 1) for k in range(X_PARTS)]
    assert d % (X_PARTS * LANES) == 0 and all(0 < s < nj for s in switch)

    def x_part(k):
        def index(i, j):
            return (jnp.minimum(i + jnp.where(j >= switch[k], 1, 0), ni - 1), k)
        return pl.BlockSpec((bm, d // X_PARTS), index)

    vmem = 2 * bm * d * 4 + 2 * bn * d * 4 + 2 * bm * bn * 2 + bm * d * 2 + bm * bn * 4 + d * bn * 2
    return pl.pallas_call(
        _norm_proj_kernel,
        out_shape=(jax.ShapeDtypeStruct((m, n), BF16), jax.ShapeDtypeStruct((m, LANES), BF16)),
        grid=(ni, nj),
        in_specs=[x_part(k) for k in range(X_PARTS)] + [
            pl.BlockSpec((1, d), lambda i, j: (0, 0)),
            pl.BlockSpec((pl.Element(bn), pl.Element(d)), w_rows),
            pl.BlockSpec((pl.Element(rank), pl.Element(d)), lambda i, j: (lr0, 0)),
        ],
        out_specs=(pl.BlockSpec((bm, bn), lambda i, j: (i, j)),
                   pl.BlockSpec((bm, LANES), lambda i, j: (i, 0))),
        scratch_shapes=[pltpu.VMEM((bm, d), BF16)],
        compiler_params=_params(("arbitrary", "arbitrary"), vmem + (8 << 20)),
        name="norm_proj",
    )(*([x2] * X_PARTS), gain, wt, wt)


def _gla_kernel(*refs, heads, dk, dv, ts, n_cast):
    q_ref, k_ref, v_ref, og_ref, alr_ref, wup_ref, bup_ref, hg_ref = refs[:8]
    o_ref = refs[8 + n_cast]
    state_ref = refs[-1]
    _cast_slabs(refs[8:8 + n_cast], refs[9 + n_cast:9 + 2 * n_cast])

    @pl.when(pl.program_id(1) == 0)
    def _():
        state_ref[...] = jnp.zeros_like(state_ref)

    w_up = wup_ref[...]
    rank = w_up.shape[0]
    w_up = jnp.concatenate([w_up, jnp.zeros((LANES - rank, w_up.shape[1]), F32)], axis=0)
    z = _dot(alr_ref[...], w_up.astype(BF16)) + bup_ref[...]
    log_a = (jnp.minimum(z, 0.0) - jnp.log(1.0 + jnp.exp(-jnp.abs(z)))) * (1.0 / GLA_TAU)

    row = lax.broadcasted_iota(jnp.int32, (ts, ts), 0)
    col = lax.broadcasted_iota(jnp.int32, (ts, ts), 1)
    tri = ((row // CHUNK == col // CHUNK) & (col <= row)).astype(BF16)
    la_hi = log_a.astype(BF16)
    la_lo = (log_a - la_hi.astype(F32)).astype(BF16)
    b = _dot(tri, la_hi) + _dot(tri, la_lo)

    crow = lax.broadcasted_iota(jnp.int32, (CHUNK, CHUNK), 0)
    ccol = lax.broadcasted_iota(jnp.int32, (CHUNK, CHUNK), 1)
    lower = ccol <= crow

    scale = dk ** -0.5
    gain = hg_ref[...]
    states = [state_ref[h] for h in range(heads)]
    for c in range(ts // CHUNK):
        r = slice(c * CHUNK, (c + 1) * CHUNK)
        bc = b[r, :]
        mid = bc[CHUNK // 2:CHUNK // 2 + 1, :]
        last = bc[CHUNK - 1:CHUNK, :]
        e_fwd = jnp.exp(bc - mid)
        e_rev = jnp.exp(mid - bc)
        qc = q_ref[r, :].astype(F32) * scale
        kc = k_ref[r, :].astype(F32)
        qe = qc * e_fwd
        ke = kc * e_rev
        q_f = qe.astype(BF16)
        k_f = ke.astype(BF16)
        q_r = (qc * e_rev).astype(BF16)
        k_r = (kc * e_fwd).astype(BF16)
        k_l = (ke * jnp.exp(last - mid)).astype(BF16)
        q_i = (qe * jnp.exp(mid)).astype(BF16)
        decay = jnp.exp(last)
        for h in range(heads):
            ck = slice(h * dk, (h + 1) * dk)
            cv = slice(h * dv, (h + 1) * dv)
            vh = v_ref[r, cv]
            att = jnp.where(lower, _dot_nt(q_f[:, ck], k_f[:, ck]), _dot_nt(q_r[:, ck], k_r[:, ck]))
            av = _dot(jnp.concatenate([att.astype(BF16), k_l[:, ck].T], axis=0), vh)
            st = states[h]
            o = av[:CHUNK] + _dot(q_i[:, ck], st.astype(BF16))
            dcol = jnp.transpose(jnp.broadcast_to(decay[:, ck], (dk, dk)))
            states[h] = st * jnp.concatenate([dcol] * (dv // dk), axis=1) + av[CHUNK:]
            o = o * lax.rsqrt(jnp.mean(o * o, axis=-1, keepdims=True) + EPS) * gain
            og = og_ref[r, cv].astype(F32)
            o_ref[r, cv] = (o * (og * jax.nn.sigmoid(og))).astype(o_ref.dtype)
    for h in range(heads):
        state_ref[h] = states[h]


def _gla(proj, a_lr, w_up, b_up, head_gain, cast_weights, *, batch, seq, heads, dk, dv, cols, ts):
    m = proj.shape[0]
    ns = seq // ts
    dkt, dvt = heads * dk, heads * dv
    rank_pad = a_lr.shape[1]

    def rows(b, s):
        return b * ns + s

    c_in, c_out, c_shapes, c_vmem = _cast_plan(cast_weights, batch * ns, rows)
    kernel = functools.partial(_gla_kernel, heads=heads, dk=dk, dv=dv, ts=ts, n_cast=len(c_in))
    vmem = (2 * ts * (2 * dkt + 2 * dvt + rank_pad) * 2 + 2 * ts * dvt * 2 + heads * dk * dv * 4
            + c_vmem)
    return pl.pallas_call(
        kernel,
        out_shape=[jax.ShapeDtypeStruct((m, dvt), BF16)] + c_shapes,
        grid=(batch, ns),
        in_specs=[
            pl.BlockSpec((ts, dkt), lambda b, s: (rows(b, s), cols["q"] // dkt)),
            pl.BlockSpec((ts, dkt), lambda b, s: (rows(b, s), cols["k"] // dkt)),
            pl.BlockSpec((ts, dvt), lambda b, s: (rows(b, s), cols["v"] // dvt)),
            pl.BlockSpec((ts, dvt), lambda b, s: (rows(b, s), cols["og"] // dvt)),
            pl.BlockSpec((ts, rank_pad), lambda b, s: (rows(b, s), 0)),
            _resident(w_up.shape),
            _resident(b_up.shape),
            _resident(head_gain.shape),
        ] + c_in,
        out_specs=[pl.BlockSpec((ts, dvt), lambda b, s: (rows(b, s), 0))] + c_out,
        scratch_shapes=[pltpu.VMEM((heads, dk, dv), F32)],
        compiler_params=_params(("arbitrary", "arbitrary"), vmem + (24 << 20)),
        name="gla",
    )(proj, proj, proj, proj, a_lr, w_up, b_up, head_gain, *cast_weights)


def _mix_out_kernel(*refs, tiles_per_seq, n_cast):
    (ax_ref, ab_ref, ac_ref, axp_ref, acp_ref, o_ref, ga_ref, gb_ref, x_ref,
     cw_ref, wa_ref, wb_ref, wm_ref, g_ref, gnext_ref) = refs[:15]
    out_ref, hnext_ref = refs[15 + n_cast:17 + n_cast]
    _cast_slabs(refs[15:15 + n_cast], refs[17 + n_cast:])

    u = ac_ref[...].astype(F32) * ax_ref[...].astype(F32)
    u_prev = acp_ref[...].astype(F32) * axp_ref[...].astype(F32)
    first = (pl.program_id(0) % tiles_per_seq) == 0
    u_prev = jnp.where(first, 0.0, u_prev)
    p1 = u_prev[SUBLANES - 1:SUBLANES, :]
    p2 = u_prev[SUBLANES - 2:SUBLANES - 1, :]
    row = lax.broadcasted_iota(jnp.int32, u.shape, 0)
    u1 = jnp.where(row == 0, p1, pltpu.roll(u, 1, 0))
    u2 = jnp.where(row == 0, p2, jnp.where(row == 1, p1, pltpu.roll(u, 2, 0)))
    cw = cw_ref[...]
    y = cw[0:1, :] * u2 + cw[1:2, :] * u1 + cw[2:3, :] * u
    ca = (ab_ref[...].astype(F32) * y).astype(BF16)

    ya = _dot(ca, wa_ref[...])
    yb = _dot(o_ref[...], wb_ref[...])
    mix = jax.nn.sigmoid(ga_ref[...].astype(F32)) * ya + jax.nn.sigmoid(gb_ref[...].astype(F32)) * yb
    m = _dot(mix.astype(BF16), wm_ref[...])
    x_new = x_ref[...] + _rms(m, g_ref[...])
    out_ref[...] = x_new
    hnext_ref[...] = _rms(x_new, gnext_ref[...]).astype(hnext_ref.dtype)


def _mix_out(proj, o, x2, conv_w, wa, wb, wm, gain, gain_next, cast_weights, *, seq, cols, bm):
    m, d = x2.shape
    ca = conv_w.shape[1]
    cb = o.shape[1]
    tiles_per_seq = seq // bm
    halo = bm // SUBLANES

    def prev(i):
        return jnp.maximum(i * halo - 1, 0)

    c_in, c_out, c_shapes, c_vmem = _cast_plan(cast_weights, m // bm, lambda i: i)
    kernel = functools.partial(_mix_out_kernel, tiles_per_seq=tiles_per_seq, n_cast=len(c_in))
    vmem = (2 * bm * (3 * ca + cb + 3 * d) * 2 + 4 * bm * d * 4
            + (ca + cb + d) * d * 2 + 4 * bm * d * 4 + c_vmem)
    return pl.pallas_call(
        kernel,
        out_shape=[jax.ShapeDtypeStruct((m, d), F32), jax.ShapeDtypeStruct((m, d), BF16)] + c_shapes,
        grid=(m // bm,),
        in_specs=[
            pl.BlockSpec((bm, ca), lambda i: (i, cols["a_x"] // ca)),
            pl.BlockSpec((bm, ca), lambda i: (i, cols["a_b"] // ca)),
            pl.BlockSpec((bm, ca), lambda i: (i, cols["a_c"] // ca)),
            pl.BlockSpec((SUBLANES, ca), lambda i: (prev(i), cols["a_x"] // ca)),
            pl.BlockSpec((SUBLANES, ca), lambda i: (prev(i), cols["a_c"] // ca)),
            pl.BlockSpec((bm, cb), lambda i: (i, 0)),
            pl.BlockSpec((bm, d), lambda i: (i, cols["gate_a"] // d)),
            pl.BlockSpec((bm, d), lambda i: (i, cols["gate_b"] // d)),
            pl.BlockSpec((bm, d), lambda i: (i, 0)),
            _resident(conv_w.shape),
            _resident(wa.shape),
            _resident(wb.shape),
            _resident(wm.shape),
            _resident(gain.shape),
            _resident(gain_next.shape),
        ] + c_in,
        out_specs=[pl.BlockSpec((bm, d), lambda i: (i, 0))] * 2 + c_out,
        compiler_params=_params(("arbitrary",), vmem + (8 << 20)),
        name="mix_out",
    )(proj, proj, proj, proj, proj, o, proj, proj, x2, conv_w, wa, wb, wm, gain, gain_next,
      *cast_weights)


def _ffn_kernel(*refs, n_cast):
    h_ref, wg_ref, wu_ref, wd_ref, gpost_ref = refs[:5]
    out_ref = refs[5 + n_cast]
    _cast_slabs(refs[5:5 + n_cast], refs[6 + n_cast:])
    j = pl.program_id(1)

    @pl.when(j == 0)
    def _():
        out_ref[...] = jnp.zeros_like(out_ref)

    h = h_ref[...]
    g = _dot(h, wg_ref[...])
    u = _dot(h, wu_ref[...])
    a = (g * jax.nn.sigmoid(g) * u).astype(BF16)
    out_ref[...] += _dot(a, wd_ref[...])

    @pl.when(j == pl.num_programs(1) - 1)
    def _():
        out_ref[...] = _rms(out_ref[...], gpost_ref[...])


def _ffn(h, wg, wu, wd, g_post, cast_weights, *, bm, bf):
    m, d = h.shape
    f = wg.shape[1]
    ni, nj = m // bm, f // bf
    per_tile = 1 << (nj.bit_length() - 1)
    c_in, c_out, c_shapes, c_vmem = _cast_plan(
        cast_weights, ni * per_tile, lambda i, j: i * per_tile + jnp.minimum(j, per_tile - 1))
    vmem = 2 * bm * d * 2 + 2 * bm * d * 4 + 2 * 3 * d * bf * 2 + 3 * bm * bf * 4 + c_vmem
    return pl.pallas_call(
        functools.partial(_ffn_kernel, n_cast=len(c_in)),
        out_shape=[jax.ShapeDtypeStruct((m, d), F32)] + c_shapes,
        grid=(ni, nj),
        in_specs=[
            pl.BlockSpec((bm, d), lambda i, j: (i, 0)),
            pl.BlockSpec((d, bf), lambda i, j: (0, j)),
            pl.BlockSpec((d, bf), lambda i, j: (0, j)),
            pl.BlockSpec((bf, d), lambda i, j: (j, 0)),
            pl.BlockSpec((1, d), lambda i, j: (0, 0)),
        ] + c_in,
        out_specs=[pl.BlockSpec((bm, d), lambda i, j: (i, 0))] + c_out,
        compiler_params=_params(("arbitrary", "arbitrary"), vmem + (8 << 20)),
        name="ffn",
    )(h, wg, wu, wd, g_post, *cast_weights)


def _ple_kernel(x_ref, f_ref, p_ref, gpre_ref, wpg_ref, wpp_ref, gpost_ref, out_ref):
    x = x_ref[...] + f_ref[...]
    h = _rms(x, gpre_ref[...]).astype(BF16)
    e = jax.nn.sigmoid(_dot(h, wpg_ref[...])) * _dot(p_ref[...].astype(BF16),
                                                      wpp_ref[...].astype(BF16))
    out_ref[...] = x + _rms(e, gpost_ref[...])


def _ple(x2, f2, p2, g_pre, wpg, wpp, g_post, *, bm):
    m, d = x2.shape
    pd = p2.shape[1]
    vmem = 6 * bm * d * 4 + 2 * bm * pd * 4 + d * d * 2 + pd * d * 4 + 4 * bm * d * 4
    return pl.pallas_call(
        _ple_kernel,
        out_shape=jax.ShapeDtypeStruct((m, d), F32),
        grid=(m // bm,),
        in_specs=[
            pl.BlockSpec((bm, d), lambda i: (i, 0)),
            pl.BlockSpec((bm, d), lambda i: (i, 0)),
            pl.BlockSpec((bm, pd), lambda i: (i, 0)),
            _resident(g_pre.shape),
            _resident(wpg.shape),
            _resident(wpp.shape),
            _resident(g_post.shape),
        ],
        out_specs=pl.BlockSpec((bm, d), lambda i: (i, 0)),
        compiler_params=_params(("parallel",), vmem + (8 << 20)),
        name="ple",
    )(x2, f2, p2, g_pre, wpg, wpp, g_post)


def _proj_layout(mix_a, dk_total, mix_b, d):
    names = ["a_x", "a_b", "a_c", "q", "k", "v", "og", "gate_a", "gate_b"]
    widths = [mix_a, mix_a, mix_a, dk_total, dk_total, mix_b, mix_b, d, d]
    cols, off = {}, 0
    for nme, wdt in zip(names, widths):
        assert off % wdt == 0, (nme, off, wdt)
        cols[nme] = off
        off += wdt
    return cols, cols["gate_a"]


def kernel(x, p, w_in, conv_w, w_a_out, w_alpha_up, b_alpha_up, gla_head_gain, w_b_out, w_mix_out, g_pre_mix, g_post_mix, g_pre_ffn, g_post_ffn, w_ff_gate, w_ff_up, w_ff_down, g_pre_ple, g_post_ple, w_ple_gate, w_ple_proj):
    batch, seq, d = x.shape
    depth = w_in.shape[0]
    mix_a = conv_w.shape[-1]
    rank, dk_total = w_alpha_up.shape[1], w_alpha_up.shape[2]
    dv = gla_head_gain.shape[-1]
    mix_b = w_b_out.shape[1]
    heads = mix_b // dv
    dk = dk_total // heads
    assert rank <= LANES and seq % CHUNK == 0
    cols, lr0 = _proj_layout(mix_a, dk_total, mix_b, d)

    xs = x.reshape(batch * seq, d)
    for i in range(depth):
        proj, a_lr = _norm_proj(xs, g_pre_mix[i][None], w_in[i].T, lr0=lr0, rank=rank,
                                bm=1024, bn=1024)
        o, wa, wb, wm, wg = _gla(
            proj, a_lr, w_alpha_up[i], b_alpha_up[i][None], gla_head_gain[i][None],
            [w_a_out[i], w_b_out[i], w_mix_out[i], w_ff_gate[i]],
            batch=batch, seq=seq, heads=heads, dk=dk, dv=dv, cols=cols, ts=256)
        xs, h_ffn, wu, wd = _mix_out(
            proj, o, xs, conv_w[i], wa, wb, wm, g_post_mix[i][None], g_pre_ffn[i][None],
            [w_ff_up[i], w_ff_down[i]], seq=seq, cols=cols, bm=256)
        f_norm, wpg = _ffn(h_ffn, wg, wu, wd, g_post_ffn[i][None], [w_ple_gate[i]],
                           bm=1024, bf=512)
        xs = _ple(xs, f_norm, p[i].reshape(batch * seq, -1), g_pre_ple[i][None], wpg,
                  w_ple_proj[i], g_post_ple[i][None], bm=512)
    return xs.reshape(batch, seq, d)
```

```python
import functools

import jax
import jax.numpy as jnp
from jax import lax
from jax.experimental import pallas as pl
from jax.experimental.pallas import tpu as pltpu

EPS = 1e-6
CHUNK = 64
GLA_TAU = 16.0
CONV_WIDTH = 3
LANES = 128
SUBLANES = 8
VMEM_LIMIT_CAP = 60000 * 1024

F32 = jnp.float32
BF16 = jnp.bfloat16


def _rms(x, gain):
    ms = jnp.mean(x * x, axis=-1, keepdims=True)
    return x * lax.rsqrt(ms + EPS) * gain


def _dot(a, b):
    return jnp.dot(a, b, preferred_element_type=F32)


def _dot_nt(a, b):
    return lax.dot_general(a, b, (((1,), (1,)), ((), ())), preferred_element_type=F32)


def _params(semantics, vmem_bytes):
    return pltpu.CompilerParams(
        dimension_semantics=semantics,
        vmem_limit_bytes=min(int(vmem_bytes), VMEM_LIMIT_CAP),
    )


def _resident(shape):
    return pl.BlockSpec(shape, lambda *_: (0,) * len(shape), pipeline_mode=pl.Buffered(1))


BF16_ROWS = 16


def _cast_plan(weights, n_slabs, slab_of):
    in_specs, out_specs, out_shapes, vmem = [], [], [], 0
    for w in weights:
        rows, width = w.shape
        assert rows % (n_slabs * BF16_ROWS) == 0, (w.shape, n_slabs)
        slab = (rows // n_slabs, width)
        spec = pl.BlockSpec(slab, lambda *idx: (slab_of(*idx), 0))
        in_specs.append(spec)
        out_specs.append(spec)
        out_shapes.append(jax.ShapeDtypeStruct(w.shape, BF16))
        vmem += 2 * slab[0] * slab[1] * (4 + 2)
    return in_specs, out_specs, out_shapes, vmem


def _cast_slabs(src_refs, dst_refs):
    for src, dst in zip(src_refs, dst_refs):
        dst[...] = src[...].astype(dst.dtype)


X_PARTS = 4


def _norm_proj_kernel(*refs):
    x_refs = refs[:X_PARTS]
    g_ref, wt_ref, wlr_ref, o_ref, lr_ref, h_ref = refs[X_PARTS:]

    @pl.when(pl.program_id(1) == 0)
    def _():
        d = h_ref.shape[1]
        w = d // X_PARTS
        ms = sum(jnp.sum(r[...] * r[...], axis=-1, keepdims=True) for r in x_refs) * (1.0 / d)
        rs = lax.rsqrt(ms + EPS)
        for k, r in enumerate(x_refs):
            c = slice(k * w, (k + 1) * w)
            h_ref[:, c] = (r[...] * rs * g_ref[:, c]).astype(BF16)
        wlr = wlr_ref[...]
        wlr = jnp.concatenate([wlr, jnp.zeros((LANES - wlr.shape[0], wlr.shape[1]), F32)], axis=0)
        lr_ref[...] = _dot_nt(h_ref[...], wlr.astype(BF16)).astype(lr_ref.dtype)

    o_ref[...] = _dot_nt(h_ref[...], wt_ref[...].astype(BF16)).astype(o_ref.dtype)


def _norm_proj(x2, gain, wt, *, lr0, rank, bm, bn):
    m, d = x2.shape
    n = wt.shape[0] - rank
    assert lr0 % bn == 0 and n % bn == 0 and rank % SUBLANES == 0

    def w_rows(i, j):
        skip = jnp.where(j * bn >= lr0, rank // SUBLANES, 0)
        return ((j * (bn // SUBLANES) + skip) * SUBLANES, 0)

    ni, nj = m // bm, n // bn
    switch = [(k + 1) * nj // (X_PARTS + 1) for k in range(X_PARTS)]
    assert d % (X_PARTS * LANES) == 0 and all(0 < s < nj for s in switch)

    def x_part(k):
        def index(i, j):
            return (jnp.minimum(i + jnp.where(j >= switch[k], 1, 0), ni - 1), k)
        return pl.BlockSpec((bm, d // X_PARTS), index)

    vmem = 2 * bm * d * 4 + 2 * bn * d * 4 + 2 * bm * bn * 2 + bm * d * 2 + bm * bn * 4 + d * bn * 2
    return pl.pallas_call(
        _norm_proj_kernel,
        out_shape=(jax.ShapeDtypeStruct((m, n), BF16), jax.ShapeDtypeStruct((m, LANES), BF16)),
        grid=(ni, nj),
        in_specs=[x_part(k) for k in range(X_PARTS)] + [
            pl.BlockSpec((1, d), lambda i, j: (0, 0)),
            pl.BlockSpec((pl.Element(bn), pl.Element(d)), w_rows),
            pl.BlockSpec((pl.Element(rank), pl.Element(d)), lambda i, j: (lr0, 0)),
        ],
        out_specs=(pl.BlockSpec((bm, bn), lambda i, j: (i, j)),
                   pl.BlockSpec((bm, LANES), lambda i, j: (i, 0))),
        scratch_shapes=[pltpu.VMEM((bm, d), BF16)],
        compiler_params=_params(("arbitrary", "arbitrary"), vmem + (8 << 20)),
        name="norm_proj",
    )(*([x2] * X_PARTS), gain, wt, wt)


def _gla_kernel(*refs, heads, dk, dv, ts, n_cast):
    q_ref, k_ref, v_ref, og_ref, alr_ref, wup_ref, bup_ref, hg_ref = refs[:8]
    o_ref = refs[8 + n_cast]
    state_ref = refs[-1]
    _cast_slabs(refs[8:8 + n_cast], refs[9 + n_cast:9 + 2 * n_cast])

    @pl.when(pl.program_id(1) == 0)
    def _():
        state_ref[...] = jnp.zeros_like(state_ref)

    w_up = wup_ref[...]
    rank = w_up.shape[0]
    w_up = jnp.concatenate([w_up, jnp.zeros((LANES - rank, w_up.shape[1]), F32)], axis=0)
    z = _dot(alr_ref[...], w_up.astype(BF16)) + bup_ref[...]
    log_a = (jnp.minimum(z, 0.0) - jnp.log(1.0 + jnp.exp(-jnp.abs(z)))) * (1.0 / GLA_TAU)

    row = lax.broadcasted_iota(jnp.int32, (ts, ts), 0)
    col = lax.broadcasted_iota(jnp.int32, (ts, ts), 1)
    tri = ((row // CHUNK == col // CHUNK) & (col <= row)).astype(BF16)
    la_hi = log_a.astype(BF16)
    la_lo = (log_a - la_hi.astype(F32)).astype(BF16)
    b = _dot(tri, la_hi) + _dot(tri, la_lo)

    crow = lax.broadcasted_iota(jnp.int32, (CHUNK, CHUNK), 0)
    ccol = lax.broadcasted_iota(jnp.int32, (CHUNK, CHUNK), 1)
    lower = ccol <= crow

    scale = dk ** -0.5
    gain = hg_ref[...]
    states = [state_ref[h] for h in range(heads)]
    for c in range(ts // CHUNK):
        r = slice(c * CHUNK, (c + 1) * CHUNK)
        bc = b[r, :]
        mid = bc[CHUNK // 2:CHUNK // 2 + 1, :]
        last = bc[CHUNK - 1:CHUNK, :]
        e_fwd = jnp.exp(bc - mid)
        e_rev = jnp.exp(mid - bc)
        qc = q_ref[r, :].astype(F32) * scale
        kc = k_ref[r, :].astype(F32)
        qe = qc * e_fwd
        ke = kc * e_rev
        q_f = qe.astype(BF16)
        k_f = ke.astype(BF16)
        q_r = (qc * e_rev).astype(BF16)
        k_r = (kc * e_fwd).astype(BF16)
        k_l = (ke * jnp.exp(last - mid)).astype(BF16)
        q_i = (qe * jnp.exp(mid)).astype(BF16)
        decay = jnp.exp(last)
        for h in range(heads):
            ck = slice(h * dk, (h + 1) * dk)
            cv = slice(h * dv, (h + 1) * dv)
            vh = v_ref[r, cv]
            att = jnp.where(lower, _dot_nt(q_f[:, ck], k_f[:, ck]), _dot_nt(q_r[:, ck], k_r[:, ck]))
            av = _dot(jnp.concatenate([att.astype(BF16), k_l[:, ck].T], axis=0), vh)
            st = states[h]
            o = av[:CHUNK] + _dot(q_i[:, ck], st.astype(BF16))
            dcol = jnp.transpose(jnp.broadcast_to(decay[:, ck], (dk, dk)))
            states[h] = st * jnp.concatenate([dcol] * (dv // dk), axis=1) + av[CHUNK:]
            o = o * lax.rsqrt(jnp.mean(o * o, axis=-1, keepdims=True) + EPS) * gain
            og = og_ref[r, cv].astype(F32)
            o_ref[r, cv] = (o * (og * jax.nn.sigmoid(og))).astype(o_ref.dtype)
    for h in range(heads):
        state_ref[h] = states[h]


def _gla(proj, a_lr, w_up, b_up, head_gain, cast_weights, *, batch, seq, heads, dk, dv, cols, ts):
    m = proj.shape[0]
    ns = seq // ts
    dkt, dvt = heads * dk, heads * dv
    rank_pad = a_lr.shape[1]

    def rows(b, s):
        return b * ns + s

    c_in, c_out, c_shapes, c_vmem = _cast_plan(cast_weights, batch * ns, rows)
    kernel = functools.partial(_gla_kernel, heads=heads, dk=dk, dv=dv, ts=ts, n_cast=len(c_in))
    vmem = (2 * ts * (2 * dkt + 2 * dvt + rank_pad) * 2 + 2 * ts * dvt * 2 + heads * dk * dv * 4
            + c_vmem)
    return pl.pallas_call(
        kernel,
        out_shape=[jax.ShapeDtypeStruct((m, dvt), BF16)] + c_shapes,
        grid=(batch, ns),
        in_specs=[
            pl.BlockSpec((ts, dkt), lambda b, s: (rows(b, s), cols["q"] // dkt)),
            pl.BlockSpec((ts, dkt), lambda b, s: (rows(b, s), cols["k"] // dkt)),
            pl.BlockSpec((ts, dvt), lambda b, s: (rows(b, s), cols["v"] // dvt)),
            pl.BlockSpec((ts, dvt), lambda b, s: (rows(b, s), cols["og"] // dvt)),
            pl.BlockSpec((ts, rank_pad), lambda b, s: (rows(b, s), 0)),
            _resident(w_up.shape),
            _resident(b_up.shape),
            _resident(head_gain.shape),
        ] + c_in,
        out_specs=[pl.BlockSpec((ts, dvt), lambda b, s: (rows(b, s), 0))] + c_out,
        scratch_shapes=[pltpu.VMEM((heads, dk, dv), F32)],
        compiler_params=_params(("arbitrary", "arbitrary"), vmem + (24 << 20)),
        name="gla",
    )(proj, proj, proj, proj, a_lr, w_up, b_up, head_gain, *cast_weights)


def _mix_out_kernel(*refs, tiles_per_seq, n_cast):
    (ax_ref, ab_ref, ac_ref, axp_ref, acp_ref, o_ref, ga_ref, gb_ref, x_ref,
     cw_ref, wa_ref, wb_ref, wm_ref, g_ref, gnext_ref) = refs[:15]
    out_ref, hnext_ref = refs[15 + n_cast:17 + n_cast]
    _cast_slabs(refs[15:15 + n_cast], refs[17 + n_cast:])

    u = ac_ref[...].astype(F32) * ax_ref[...].astype(F32)
    u_prev = acp_ref[...].astype(F32) * axp_ref[...].astype(F32)
    first = (pl.program_id(0) % tiles_per_seq) == 0
    u_prev = jnp.where(first, 0.0, u_prev)
    p1 = u_prev[SUBLANES - 1:SUBLANES, :]
    p2 = u_prev[SUBLANES - 2:SUBLANES - 1, :]
    row = lax.broadcasted_iota(jnp.int32, u.shape, 0)
    u1 = jnp.where(row == 0, p1, pltpu.roll(u, 1, 0))
    u2 = jnp.where(row == 0, p2, jnp.where(row == 1, p1, pltpu.roll(u, 2, 0)))
    cw = cw_ref[...]
    y = cw[0:1, :] * u2 + cw[1:2, :] * u1 + cw[2:3, :] * u
    ca = (ab_ref[...].astype(F32) * y).astype(BF16)

    ya = _dot(ca, wa_ref[...])
    yb = _dot(o_ref[...], wb_ref[...])
    mix = jax.nn.sigmoid(ga_ref[...].astype(F32)) * ya + jax.nn.sigmoid(gb_ref[...].astype(F32)) * yb
    m = _dot(mix.astype(BF16), wm_ref[...])
    x_new = x_ref[...] + _rms(m, g_ref[...])
    out_ref[...] = x_new
    hnext_ref[...] = _rms(x_new, gnext_ref[...]).astype(hnext_ref.dtype)


def _mix_out(proj, o, x2, conv_w, wa, wb, wm, gain, gain_next, cast_weights, *, seq, cols, bm):
    m, d = x2.shape
    ca = conv_w.shape[1]
    cb = o.shape[1]
    tiles_per_seq = seq // bm
    halo = bm // SUBLANES

    def prev(i):
        return jnp.maximum(i * halo - 1, 0)

    c_in, c_out, c_shapes, c_vmem = _cast_plan(cast_weights, m // bm, lambda i: i)
    kernel = functools.partial(_mix_out_kernel, tiles_per_seq=tiles_per_seq, n_cast=len(c_in))
    vmem = (2 * bm * (3 * ca + cb + 3 * d) * 2 + 4 * bm * d * 4
            + (ca + cb + d) * d * 2 + 4 * bm * d * 4 + c_vmem)
    return pl.pallas_call(
        kernel,
        out_shape=[jax.ShapeDtypeStruct((m, d), F32), jax.ShapeDtypeStruct((m, d), BF16)] + c_shapes,
        grid=(m // bm,),
        in_specs=[
            pl.BlockSpec((bm, ca), lambda i: (i, cols["a_x"] // ca)),
            pl.BlockSpec((bm, ca), lambda i: (i, cols["a_b"] // ca)),
            pl.BlockSpec((bm, ca), lambda i: (i, cols["a_c"] // ca)),
            pl.BlockSpec((SUBLANES, ca), lambda i: (prev(i), cols["a_x"] // ca)),
            pl.BlockSpec((SUBLANES, ca), lambda i: (prev(i), cols["a_c"] // ca)),
            pl.BlockSpec((bm, cb), lambda i: (i, 0)),
            pl.BlockSpec((bm, d), lambda i: (i, cols["gate_a"] // d)),
            pl.BlockSpec((bm, d), lambda i: (i, cols["gate_b"] // d)),
            pl.BlockSpec((bm, d), lambda i: (i, 0)),
            _resident(conv_w.shape),
            _resident(wa.shape),
            _resident(wb.shape),
            _resident(wm.shape),
            _resident(gain.shape),
            _resident(gain_next.shape),
        ] + c_in,
        out_specs=[pl.BlockSpec((bm, d), lambda i: (i, 0))] * 2 + c_out,
        compiler_params=_params(("arbitrary",), vmem + (8 << 20)),
        name="mix_out",
    )(proj, proj, proj, proj, proj, o, proj, proj, x2, conv_w, wa, wb, wm, gain, gain_next,
      *cast_weights)


def _ffn_kernel(h_ref, wg_ref, wu_ref, wd_ref, gpost_ref, out_ref, acc_ref):
    j = pl.program_id(1)

    @pl.when(j == 0)
    def _():
        acc_ref[...] = jnp.zeros_like(acc_ref)

    h = h_ref[...]
    g = _dot(h, wg_ref[...])
    u = _dot(h, wu_ref[...])
    a = (g * jax.nn.sigmoid(g) * u).astype(BF16)
    acc_ref[...] += _dot(a, wd_ref[...])

    @pl.when(j == pl.num_programs(1) - 1)
    def _():
        out_ref[...] = _rms(acc_ref[...], gpost_ref[...]).astype(out_ref.dtype)


def _ffn(h, wg, wu, wd, g_post, *, bm, bf):
    m, d = h.shape
    f = wg.shape[1]
    vmem = 2 * bm * d * 2 + bm * d * 4 + 2 * bm * d * 2 + 2 * 3 * d * bf * 2 + 3 * bm * bf * 4
    return pl.pallas_call(
        _ffn_kernel,
        out_shape=jax.ShapeDtypeStruct((m, d), BF16),
        grid=(m // bm, f // bf),
        in_specs=[
            pl.BlockSpec((bm, d), lambda i, j: (i, 0)),
            pl.BlockSpec((d, bf), lambda i, j: (0, j)),
            pl.BlockSpec((d, bf), lambda i, j: (0, j)),
            pl.BlockSpec((bf, d), lambda i, j: (j, 0)),
            pl.BlockSpec((1, d), lambda i, j: (0, 0)),
        ],
        out_specs=pl.BlockSpec((bm, d), lambda i, j: (i, 0)),
        scratch_shapes=[pltpu.VMEM((bm, d), F32)],
        compiler_params=_params(("parallel", "arbitrary"), vmem + (8 << 20)),
        name="ffn",
    )(h, wg, wu, wd, g_post)


def _ple_kernel(x_ref, f_ref, p_ref, gpre_ref, wpg_ref, wpp_ref, gpost_ref, out_ref):
    x = x_ref[...] + f_ref[...].astype(F32)
    h = _rms(x, gpre_ref[...]).astype(BF16)
    e = jax.nn.sigmoid(_dot(h, wpg_ref[...])) * _dot(p_ref[...].astype(BF16),
                                                      wpp_ref[...].astype(BF16))
    out_ref[...] = x + _rms(e, gpost_ref[...])


def _ple(x2, f2, p2, g_pre, wpg, wpp, g_post, *, bm):
    m, d = x2.shape
    pd = p2.shape[1]
    vmem = 5 * bm * d * 4 + 2 * bm * pd * 4 + d * d * 2 + pd * d * 4 + 4 * bm * d * 4
    return pl.pallas_call(
        _ple_kernel,
        out_shape=jax.ShapeDtypeStruct((m, d), F32),
        grid=(m // bm,),
        in_specs=[
            pl.BlockSpec((bm, d), lambda i: (i, 0)),
            pl.BlockSpec((bm, d), lambda i: (i, 0)),
            pl.BlockSpec((bm, pd), lambda i: (i, 0)),
            _resident(g_pre.shape),
            _resident(wpg.shape),
            _resident(wpp.shape),
            _resident(g_post.shape),
        ],
        out_specs=pl.BlockSpec((bm, d), lambda i: (i, 0)),
        compiler_params=_params(("parallel",), vmem + (8 << 20)),
        name="ple",
    )(x2, f2, p2, g_pre, wpg, wpp, g_post)


def _proj_layout(mix_a, dk_total, mix_b, d):
    names = ["a_x", "a_b", "a_c", "q", "k", "v", "og", "gate_a", "gate_b"]
    widths = [mix_a, mix_a, mix_a, dk_total, dk_total, mix_b, mix_b, d, d]
    cols, off = {}, 0
    for nme, wdt in zip(names, widths):
        assert off % wdt == 0, (nme, off, wdt)
        cols[nme] = off
        off += wdt
    return cols, cols["gate_a"]


def kernel(x, p, w_in, conv_w, w_a_out, w_alpha_up, b_alpha_up, gla_head_gain, w_b_out, w_mix_out, g_pre_mix, g_post_mix, g_pre_ffn, g_post_ffn, w_ff_gate, w_ff_up, w_ff_down, g_pre_ple, g_post_ple, w_ple_gate, w_ple_proj):
    batch, seq, d = x.shape
    depth = w_in.shape[0]
    mix_a = conv_w.shape[-1]
    rank, dk_total = w_alpha_up.shape[1], w_alpha_up.shape[2]
    dv = gla_head_gain.shape[-1]
    mix_b = w_b_out.shape[1]
    heads = mix_b // dv
    dk = dk_total // heads
    assert rank <= LANES and seq % CHUNK == 0
    cols, lr0 = _proj_layout(mix_a, dk_total, mix_b, d)

    xs = x.reshape(batch * seq, d)
    for i in range(depth):
        proj, a_lr = _norm_proj(xs, g_pre_mix[i][None], w_in[i].T, lr0=lr0, rank=rank,
                                bm=1024, bn=1024)
        o, wa, wb, wm = _gla(
            proj, a_lr, w_alpha_up[i], b_alpha_up[i][None], gla_head_gain[i][None],
            [w_a_out[i], w_b_out[i], w_mix_out[i]],
            batch=batch, seq=seq, heads=heads, dk=dk, dv=dv, cols=cols, ts=256)
        xs, h_ffn, wg, wu, wd, wpg = _mix_out(
            proj, o, xs, conv_w[i], wa, wb, wm, g_post_mix[i][None], g_pre_ffn[i][None],
            [w_ff_gate[i], w_ff_up[i], w_ff_down[i], w_ple_gate[i]], seq=seq, cols=cols, bm=256)
        f_norm = _ffn(h_ffn, wg, wu, wd, g_post_ffn[i][None], bm=1024, bf=512)
        xs = _ple(xs, f_norm, p[i].reshape(batch * seq, -1), g_pre_ple[i][None], wpg,
                  w_ple_proj[i], g_post_ple[i][None], bm=512)
    return xs.reshape(batch, seq, d)
```

```python
import functools

import jax
import jax.numpy as jnp
from jax import lax
from jax.experimental import pallas as pl
from jax.experimental.pallas import tpu as pltpu

EPS = 1e-6
CHUNK = 64
GLA_TAU = 16.0
CONV_WIDTH = 3
LANES = 128
SUBLANES = 8
VMEM_LIMIT_CAP = 60000 * 1024

F32 = jnp.float32
BF16 = jnp.bfloat16


def _rms(x, gain):
    ms = jnp.mean(x * x, axis=-1, keepdims=True)
    return x * lax.rsqrt(ms + EPS) * gain


def _dot(a, b):
    return jnp.dot(a, b, preferred_element_type=F32)


def _dot_nt(a, b):
    return lax.dot_general(a, b, (((1,), (1,)), ((), ())), preferred_element_type=F32)


def _params(semantics, vmem_bytes):
    return pltpu.CompilerParams(
        dimension_semantics=semantics,
        vmem_limit_bytes=min(int(vmem_bytes), VMEM_LIMIT_CAP),
    )


def _resident(shape):
    return pl.BlockSpec(shape, lambda *_: (0,) * len(shape), pipeline_mode=pl.Buffered(1))


BF16_ROWS = 16


def _cast_plan(weights, n_slabs, slab_of):
    in_specs, out_specs, out_shapes, vmem = [], [], [], 0
    for w in weights:
        rows, width = w.shape
        assert rows % (n_slabs * BF16_ROWS) == 0, (w.shape, n_slabs)
        slab = (rows // n_slabs, width)
        spec = pl.BlockSpec(slab, lambda *idx: (slab_of(*idx), 0))
        in_specs.append(spec)
        out_specs.append(spec)
        out_shapes.append(jax.ShapeDtypeStruct(w.shape, BF16))
        vmem += 2 * slab[0] * slab[1] * (4 + 2)
    return in_specs, out_specs, out_shapes, vmem


def _cast_slabs(src_refs, dst_refs):
    for src, dst in zip(src_refs, dst_refs):
        dst[...] = src[...].astype(dst.dtype)


X_PARTS = 4


def _norm_proj_kernel(*refs):
    x_refs = refs[:X_PARTS]
    g_ref, wt_ref, wlr_ref, o_ref, lr_ref, h_ref = refs[X_PARTS:]

    @pl.when(pl.program_id(1) == 0)
    def _():
        d = h_ref.shape[1]
        w = d // X_PARTS
        ms = sum(jnp.sum(r[...] * r[...], axis=-1, keepdims=True) for r in x_refs) * (1.0 / d)
        rs = lax.rsqrt(ms + EPS)
        for k, r in enumerate(x_refs):
            c = slice(k * w, (k + 1) * w)
            h_ref[:, c] = (r[...] * rs * g_ref[:, c]).astype(BF16)
        wlr = wlr_ref[...]
        wlr = jnp.concatenate([wlr, jnp.zeros((LANES - wlr.shape[0], wlr.shape[1]), F32)], axis=0)
        lr_ref[...] = _dot_nt(h_ref[...], wlr.astype(BF16)).astype(lr_ref.dtype)

    o_ref[...] = _dot_nt(h_ref[...], wt_ref[...].astype(BF16)).astype(o_ref.dtype)


def _norm_proj(x2, gain, wt, *, lr0, rank, bm, bn):
    m, d = x2.shape
    n = wt.shape[0] - rank
    assert lr0 % bn == 0 and n % bn == 0 and rank % SUBLANES == 0

    def w_rows(i, j):
        skip = jnp.where(j * bn >= lr0, rank // SUBLANES, 0)
        return ((j * (bn // SUBLANES) + skip) * SUBLANES, 0)

    ni, nj = m // bm, n // bn
    switch = [(k + 1) * nj // (X_PARTS + 1) for k in range(X_PARTS)]
    assert d % (X_PARTS * LANES) == 0 and all(0 < s < nj for s in switch)

    def x_part(k):
        def index(i, j):
            return (jnp.minimum(i + jnp.where(j >= switch[k], 1, 0), ni - 1), k)
        return pl.BlockSpec((bm, d // X_PARTS), index)

    vmem = 2 * bm * d * 4 + 2 * bn * d * 4 + 2 * bm * bn * 2 + bm * d * 2 + bm * bn * 4 + d * bn * 2
    return pl.pallas_call(
        _norm_proj_kernel,
        out_shape=(jax.ShapeDtypeStruct((m, n), BF16), jax.ShapeDtypeStruct((m, LANES), BF16)),
        grid=(ni, nj),
        in_specs=[x_part(k) for k in range(X_PARTS)] + [
            pl.BlockSpec((1, d), lambda i, j: (0, 0)),
            pl.BlockSpec((pl.Element(bn), pl.Element(d)), w_rows),
            pl.BlockSpec((pl.Element(rank), pl.Element(d)), lambda i, j: (lr0, 0)),
        ],
        out_specs=(pl.BlockSpec((bm, bn), lambda i, j: (i, j)),
                   pl.BlockSpec((bm, LANES), lambda i, j: (i, 0))),
        scratch_shapes=[pltpu.VMEM((bm, d), BF16)],
        compiler_params=_params(("arbitrary", "arbitrary"), vmem + (8 << 20)),
        name="norm_proj",
    )(*([x2] * X_PARTS), gain, wt, wt)


def _gla_kernel(*refs, heads, dk, dv, ts, n_cast):
    q_ref, k_ref, v_ref, og_ref, alr_ref, wup_ref, bup_ref, hg_ref = refs[:8]
    o_ref = refs[8 + n_cast]
    state_ref = refs[-1]
    _cast_slabs(refs[8:8 + n_cast], refs[9 + n_cast:9 + 2 * n_cast])

    @pl.when(pl.program_id(1) == 0)
    def _():
        state_ref[...] = jnp.zeros_like(state_ref)

    w_up = wup_ref[...]
    rank = w_up.shape[0]
    w_up = jnp.concatenate([w_up, jnp.zeros((LANES - rank, w_up.shape[1]), F32)], axis=0)
    z = _dot(alr_ref[...], w_up.astype(BF16)) + bup_ref[...]
    log_a = (jnp.minimum(z, 0.0) - jnp.log(1.0 + jnp.exp(-jnp.abs(z)))) * (1.0 / GLA_TAU)

    row = lax.broadcasted_iota(jnp.int32, (ts, ts), 0)
    col = lax.broadcasted_iota(jnp.int32, (ts, ts), 1)
    tri = ((row // CHUNK == col // CHUNK) & (col <= row)).astype(BF16)
    la_hi = log_a.astype(BF16)
    la_lo = (log_a - la_hi.astype(F32)).astype(BF16)
    b = _dot(tri, la_hi) + _dot(tri, la_lo)

    crow = lax.broadcasted_iota(jnp.int32, (CHUNK, CHUNK), 0)
    ccol = lax.broadcasted_iota(jnp.int32, (CHUNK, CHUNK), 1)
    lower = ccol <= crow

    scale = dk ** -0.5
    gain = hg_ref[...]
    states = [state_ref[h] for h in range(heads)]
    for c in range(ts // CHUNK):
        r = slice(c * CHUNK, (c + 1) * CHUNK)
        bc = b[r, :]
        mid = bc[CHUNK // 2:CHUNK // 2 + 1, :]
        last = bc[CHUNK - 1:CHUNK, :]
        e_fwd = jnp.exp(bc - mid)
        e_rev = jnp.exp(mid - bc)
        qc = q_ref[r, :].astype(F32) * scale
        kc = k_ref[r, :].astype(F32)
        qe = qc * e_fwd
        ke = kc * e_rev
        q_f = qe.astype(BF16)
        k_f = ke.astype(BF16)
        q_r = (qc * e_rev).astype(BF16)
        k_r = (kc * e_fwd).astype(BF16)
        k_l = (ke * jnp.exp(last - mid)).astype(BF16)
        q_i = (qe * jnp.exp(mid)).astype(BF16)
        decay = jnp.exp(last)
        for h in range(heads):
            ck = slice(h * dk, (h + 1) * dk)
            cv = slice(h * dv, (h + 1) * dv)
            vh = v_ref[r, cv]
            att = jnp.where(lower, _dot_nt(q_f[:, ck], k_f[:, ck]), _dot_nt(q_r[:, ck], k_r[:, ck]))
            av = _dot(jnp.concatenate([att.astype(BF16), k_l[:, ck].T], axis=0), vh)
            st = states[h]
            o = av[:CHUNK] + _dot(q_i[:, ck], st.astype(BF16))
            dcol = jnp.transpose(jnp.broadcast_to(decay[:, ck], (dk, dk)))
            states[h] = st * jnp.concatenate([dcol] * (dv // dk), axis=1) + av[CHUNK:]
            o = o * lax.rsqrt(jnp.mean(o * o, axis=-1, keepdims=True) + EPS) * gain
            og = og_ref[r, cv].astype(F32)
            o_ref[r, cv] = (o * (og * jax.nn.sigmoid(og))).astype(o_ref.dtype)
    for h in range(heads):
        state_ref[h] = states[h]


def _gla(proj, a_lr, w_up, b_up, head_gain, cast_weights, *, batch, seq, heads, dk, dv, cols, ts):
    m = proj.shape[0]
    ns = seq // ts
    dkt, dvt = heads * dk, heads * dv
    rank_pad = a_lr.shape[1]

    def rows(b, s):
        return b * ns + s

    c_in, c_out, c_shapes, c_vmem = _cast_plan(cast_weights, batch * ns, rows)
    kernel = functools.partial(_gla_kernel, heads=heads, dk=dk, dv=dv, ts=ts, n_cast=len(c_in))
    vmem = (2 * ts * (2 * dkt + 2 * dvt + rank_pad) * 2 + 2 * ts * dvt * 2 + heads * dk * dv * 4
            + c_vmem)
    return pl.pallas_call(
        kernel,
        out_shape=[jax.ShapeDtypeStruct((m, dvt), BF16)] + c_shapes,
        grid=(batch, ns),
        in_specs=[
            pl.BlockSpec((ts, dkt), lambda b, s: (rows(b, s), cols["q"] // dkt)),
            pl.BlockSpec((ts, dkt), lambda b, s: (rows(b, s), cols["k"] // dkt)),
            pl.BlockSpec((ts, dvt), lambda b, s: (rows(b, s), cols["v"] // dvt)),
            pl.BlockSpec((ts, dvt), lambda b, s: (rows(b, s), cols["og"] // dvt)),
            pl.BlockSpec((ts, rank_pad), lambda b, s: (rows(b, s), 0)),
            _resident(w_up.shape),
            _resident(b_up.shape),
            _resident(head_gain.shape),
        ] + c_in,
        out_specs=[pl.BlockSpec((ts, dvt), lambda b, s: (rows(b, s), 0))] + c_out,
        scratch_shapes=[pltpu.VMEM((heads, dk, dv), F32)],
        compiler_params=_params(("arbitrary", "arbitrary"), VMEM_LIMIT_CAP),
        name="gla",
    )(proj, proj, proj, proj, a_lr, w_up, b_up, head_gain, *cast_weights)


def _mix_out_kernel(*refs, tiles_per_seq, n_cast):
    (ax_ref, ab_ref, ac_ref, axp_ref, acp_ref, o_ref, ga_ref, gb_ref, x_ref,
     cw_ref, wa_ref, wb_ref, wm_ref, g_ref, gnext_ref) = refs[:15]
    out_ref, hnext_ref = refs[15 + n_cast:17 + n_cast]
    _cast_slabs(refs[15:15 + n_cast], refs[17 + n_cast:])

    u = ac_ref[...].astype(F32) * ax_ref[...].astype(F32)
    u_prev = acp_ref[...].astype(F32) * axp_ref[...].astype(F32)
    first = (pl.program_id(0) % tiles_per_seq) == 0
    u_prev = jnp.where(first, 0.0, u_prev)
    p1 = u_prev[SUBLANES - 1:SUBLANES, :]
    p2 = u_prev[SUBLANES - 2:SUBLANES - 1, :]
    row = lax.broadcasted_iota(jnp.int32, u.shape, 0)
    u1 = jnp.where(row == 0, p1, pltpu.roll(u, 1, 0))
    u2 = jnp.where(row == 0, p2, jnp.where(row == 1, p1, pltpu.roll(u, 2, 0)))
    y = cw_ref[0] * u2 + cw_ref[1] * u1 + cw_ref[2] * u
    ca = (ab_ref[...].astype(F32) * y).astype(BF16)

    ya = _dot(ca, wa_ref[...])
    yb = _dot(o_ref[...], wb_ref[...])
    mix = jax.nn.sigmoid(ga_ref[...].astype(F32)) * ya + jax.nn.sigmoid(gb_ref[...].astype(F32)) * yb
    m = _dot(mix.astype(BF16), wm_ref[...])
    x_new = x_ref[...] + _rms(m, g_ref[...])
    out_ref[...] = x_new
    hnext_ref[...] = _rms(x_new, gnext_ref[...]).astype(hnext_ref.dtype)


def _mix_out(proj, o, x2, conv_w, wa, wb, wm, gain, gain_next, cast_weights, *, seq, cols, bm):
    m, d = x2.shape
    ca = conv_w.shape[-1]
    cb = o.shape[1]
    tiles_per_seq = seq // bm
    halo = bm // SUBLANES

    def prev(i):
        return jnp.maximum(i * halo - 1, 0)

    c_in, c_out, c_shapes, c_vmem = _cast_plan(cast_weights, m // bm, lambda i: i)
    kernel = functools.partial(_mix_out_kernel, tiles_per_seq=tiles_per_seq, n_cast=len(c_in))
    vmem = (2 * bm * (3 * ca + cb + 3 * d) * 2 + 4 * bm * d * 4
            + (ca + cb + d) * d * 2 + 4 * bm * d * 4 + c_vmem)
    return pl.pallas_call(
        kernel,
        out_shape=[jax.ShapeDtypeStruct((m, d), F32), jax.ShapeDtypeStruct((m, d), BF16)] + c_shapes,
        grid=(m // bm,),
        in_specs=[
            pl.BlockSpec((bm, ca), lambda i: (i, cols["a_x"] // ca)),
            pl.BlockSpec((bm, ca), lambda i: (i, cols["a_b"] // ca)),
            pl.BlockSpec((bm, ca), lambda i: (i, cols["a_c"] // ca)),
            pl.BlockSpec((SUBLANES, ca), lambda i: (prev(i), cols["a_x"] // ca)),
            pl.BlockSpec((SUBLANES, ca), lambda i: (prev(i), cols["a_c"] // ca)),
            pl.BlockSpec((bm, cb), lambda i: (i, 0)),
            pl.BlockSpec((bm, d), lambda i: (i, cols["gate_a"] // d)),
            pl.BlockSpec((bm, d), lambda i: (i, cols["gate_b"] // d)),
            pl.BlockSpec((bm, d), lambda i: (i, 0)),
            _resident(conv_w.shape),
            _resident(wa.shape),
            _resident(wb.shape),
            _resident(wm.shape),
            _resident(gain.shape),
            _resident(gain_next.shape),
        ] + c_in,
        out_specs=[pl.BlockSpec((bm, d), lambda i: (i, 0))] * 2 + c_out,
        compiler_params=_params(("arbitrary",), vmem + (8 << 20)),
        name="mix_out",
    )(proj, proj, proj, proj, proj, o, proj, proj, x2, conv_w, wa, wb, wm, gain, gain_next,
      *cast_weights)


def _ffn_kernel(h_ref, wg_ref, wu_ref, wd_ref, gpost_ref, out_ref):
    j = pl.program_id(1)

    @pl.when(j == 0)
    def _():
        out_ref[...] = jnp.zeros_like(out_ref)

    h = h_ref[...]
    g = _dot(h, wg_ref[...])
    u = _dot(h, wu_ref[...])
    a = (g * jax.nn.sigmoid(g) * u).astype(BF16)
    out_ref[...] += _dot(a, wd_ref[...])

    @pl.when(j == pl.num_programs(1) - 1)
    def _():
        out_ref[...] = _rms(out_ref[...], gpost_ref[...])


def _ffn(h, wg, wu, wd, g_post, *, bm, bf):
    m, d = h.shape
    f = wg.shape[1]
    vmem = 2 * bm * d * 2 + 2 * bm * d * 4 + 2 * 3 * d * bf * 2 + 3 * bm * bf * 4
    return pl.pallas_call(
        _ffn_kernel,
        out_shape=jax.ShapeDtypeStruct((m, d), F32),
        grid=(m // bm, f // bf),
        in_specs=[
            pl.BlockSpec((bm, d), lambda i, j: (i, 0)),
            pl.BlockSpec((d, bf), lambda i, j: (0, j)),
            pl.BlockSpec((d, bf), lambda i, j: (0, j)),
            pl.BlockSpec((bf, d), lambda i, j: (j, 0)),
            pl.BlockSpec((1, d), lambda i, j: (0, 0)),
        ],
        out_specs=pl.BlockSpec((bm, d), lambda i, j: (i, 0)),
        compiler_params=_params(("parallel", "arbitrary"), vmem + (8 << 20)),
        name="ffn",
    )(h, wg, wu, wd, g_post)


def _ple_kernel(x_ref, f_ref, p_ref, gpre_ref, wpg_ref, wpp_ref, gpost_ref, out_ref):
    x = x_ref[...] + f_ref[...]
    h = _rms(x, gpre_ref[...]).astype(BF16)
    e = jax.nn.sigmoid(_dot(h, wpg_ref[...])) * _dot(p_ref[...].astype(BF16),
                                                      wpp_ref[...].astype(BF16))
    out_ref[...] = x + _rms(e, gpost_ref[...])


def _ple(x2, f2, p2, g_pre, wpg, wpp, g_post, *, bm):
    m, d = x2.shape
    pd = p2.shape[1]
    vmem = 6 * bm * d * 4 + 2 * bm * pd * 4 + d * d * 2 + pd * d * 4 + 4 * bm * d * 4
    return pl.pallas_call(
        _ple_kernel,
        out_shape=jax.ShapeDtypeStruct((m, d), F32),
        grid=(m // bm,),
        in_specs=[
            pl.BlockSpec((bm, d), lambda i: (i, 0)),
            pl.BlockSpec((bm, d), lambda i: (i, 0)),
            pl.BlockSpec((bm, pd), lambda i: (i, 0)),
            _resident(g_pre.shape),
            _resident(wpg.shape),
            _resident(wpp.shape),
            _resident(g_post.shape),
        ],
        out_specs=pl.BlockSpec((bm, d), lambda i: (i, 0)),
        compiler_params=_params(("parallel",), vmem + (8 << 20)),
        name="ple",
    )(x2, f2, p2, g_pre, wpg, wpp, g_post)


def _proj_layout(mix_a, dk_total, mix_b, d):
    names = ["a_x", "a_b", "a_c", "q", "k", "v", "og", "gate_a", "gate_b"]
    widths = [mix_a, mix_a, mix_a, dk_total, dk_total, mix_b, mix_b, d, d]
    cols, off = {}, 0
    for nme, wdt in zip(names, widths):
        assert off % wdt == 0, (nme, off, wdt)
        cols[nme] = off
        off += wdt
    return cols, cols["gate_a"]


def kernel(x, p, w_in, conv_w, w_a_out, w_alpha_up, b_alpha_up, gla_head_gain, w_b_out, w_mix_out, g_pre_mix, g_post_mix, g_pre_ffn, g_post_ffn, w_ff_gate, w_ff_up, w_ff_down, g_pre_ple, g_post_ple, w_ple_gate, w_ple_proj):
    batch, seq, d = x.shape
    depth = w_in.shape[0]
    mix_a = conv_w.shape[-1]
    rank, dk_total = w_alpha_up.shape[1], w_alpha_up.shape[2]
    dv = gla_head_gain.shape[-1]
    mix_b = w_b_out.shape[1]
    heads = mix_b // dv
    dk = dk_total // heads
    assert rank <= LANES and seq % CHUNK == 0
    cols, lr0 = _proj_layout(mix_a, dk_total, mix_b, d)

    xs = x.reshape(batch * seq, d)
    for i in range(depth):
        proj, a_lr = _norm_proj(xs, g_pre_mix[i][None], w_in[i].T.astype(BF16), lr0=lr0, rank=rank,
                                bm=1024, bn=1024)
        o, wa, wb, wm, wg = _gla(
            proj, a_lr, w_alpha_up[i], b_alpha_up[i][None], gla_head_gain[i][None],
            [w_a_out[i], w_b_out[i], w_mix_out[i], w_ff_gate[i]],
            batch=batch, seq=seq, heads=heads, dk=dk, dv=dv, cols=cols, ts=256)
        xs, h_ffn, wu, wd, wpg = _mix_out(
            proj, o, xs, conv_w[i][:, None, :], wa, wb, wm, g_post_mix[i][None],
            g_pre_ffn[i][None],
            [w_ff_up[i], w_ff_down[i], w_ple_gate[i]], seq=seq, cols=cols, bm=256)
        f_norm = _ffn(h_ffn, wg, wu, wd, g_post_ffn[i][None], bm=1024, bf=512)
        xs = _ple(xs, f_norm, p[i].reshape(batch * seq, -1), g_pre_ple[i][None], wpg,
                  w_ple_proj[i], g_post_ple[i][None], bm=512)
    return xs.reshape(batch, seq, d)
```

```python
import functools

import jax
import jax.numpy as jnp
from jax import lax
from jax.experimental import pallas as pl
from jax.experimental.pallas import tpu as pltpu

EPS = 1e-6
CHUNK = 64
GLA_TAU = 16.0
CONV_WIDTH = 3
LANES = 128
SUBLANES = 8
VMEM_LIMIT_CAP = 60000 * 1024
SPILL_ROOM = 8 << 20

F32 = jnp.float32
BF16 = jnp.bfloat16


def _rms(x, gain):
    ms = jnp.mean(x * x, axis=-1, keepdims=True)
    return x * lax.rsqrt(ms + EPS) * gain


def _dot(a, b):
    return jnp.dot(a, b, preferred_element_type=F32)


def _dot_nt(a, b):
    return lax.dot_general(a, b, (((1,), (1,)), ((), ())), preferred_element_type=F32)


def _params(semantics, vmem_bytes):
    return pltpu.CompilerParams(
        dimension_semantics=semantics,
        vmem_limit_bytes=min(int(vmem_bytes), VMEM_LIMIT_CAP),
    )


def _resident(shape):
    return pl.BlockSpec(shape, lambda *_: (0,) * len(shape), pipeline_mode=pl.Buffered(1))


BF16_ROWS = 16


def _cast_plan(weights, n_slabs, slab_of):
    in_specs, out_specs, out_shapes, vmem = [], [], [], 0
    for w in weights:
        rows, width = w.shape
        assert rows % (n_slabs * BF16_ROWS) == 0, (w.shape, n_slabs)
        slab = (rows // n_slabs, width)
        spec = pl.BlockSpec(slab, lambda *idx: (slab_of(*idx), 0))
        in_specs.append(spec)
        out_specs.append(spec)
        out_shapes.append(jax.ShapeDtypeStruct(w.shape, BF16))
        vmem += 2 * slab[0] * slab[1] * (4 + 2)
    return in_specs, out_specs, out_shapes, vmem


def _cast_slabs(src_refs, dst_refs):
    for src, dst in zip(src_refs, dst_refs):
        dst[...] = src[...].astype(dst.dtype)


X_PARTS = 4


def _norm_proj_kernel(*refs):
    x_refs = refs[:X_PARTS]
    g_ref, wt_ref, wlr_ref, o_ref, lr_ref, h_ref = refs[X_PARTS:]

    @pl.when(pl.program_id(1) == 0)
    def _():
        d = h_ref.shape[1]
        w = d // X_PARTS
        ms = sum(jnp.sum(r[...] * r[...], axis=-1, keepdims=True) for r in x_refs) * (1.0 / d)
        rs = lax.rsqrt(ms + EPS)
        for k, r in enumerate(x_refs):
            c = slice(k * w, (k + 1) * w)
            h_ref[:, c] = (r[...] * rs * g_ref[:, c]).astype(BF16)
        wlr = wlr_ref[...]
        wlr = jnp.concatenate([wlr, jnp.zeros((LANES - wlr.shape[0], wlr.shape[1]), F32)], axis=0)
        lr_ref[...] = _dot_nt(h_ref[...], wlr.astype(BF16)).astype(lr_ref.dtype)

    o_ref[...] = _dot_nt(h_ref[...], wt_ref[...].astype(BF16)).astype(o_ref.dtype)


def _norm_proj(x2, gain, wt, *, lr0, rank, bm, bn):
    m, d = x2.shape
    n = wt.shape[0] - rank
    assert lr0 % bn == 0 and n % bn == 0 and rank % SUBLANES == 0

    def w_rows(i, j):
        skip = jnp.where(j * bn >= lr0, rank // SUBLANES, 0)
        return ((j * (bn // SUBLANES) + skip) * SUBLANES, 0)

    ni, nj = m // bm, n // bn
    switch = [(k + 1) * nj // (X_PARTS + 1) for k in range(X_PARTS)]
    assert d % (X_PARTS * LANES) == 0 and all(0 < s < nj for s in switch)

    def x_part(k):
        def index(i, j):
            return (jnp.minimum(i + jnp.where(j >= switch[k], 1, 0), ni - 1), k)
        return pl.BlockSpec((bm, d // X_PARTS), index)

    vmem = 2 * bm * d * 4 + 2 * bn * d * 4 + 2 * bm * bn * 2 + bm * d * 2 + bm * bn * 4 + d * bn * 2
    return pl.pallas_call(
        _norm_proj_kernel,
        out_shape=(jax.ShapeDtypeStruct((m, n), BF16), jax.ShapeDtypeStruct((m, LANES), BF16)),
        grid=(ni, nj),
        in_specs=[x_part(k) for k in range(X_PARTS)] + [
            pl.BlockSpec((1, d), lambda i, j: (0, 0)),
            pl.BlockSpec((pl.Element(bn), pl.Element(d)), w_rows),
            pl.BlockSpec((pl.Element(rank), pl.Element(d)), lambda i, j: (lr0, 0)),
        ],
        out_specs=(pl.BlockSpec((bm, bn), lambda i, j: (i, j)),
                   pl.BlockSpec((bm, LANES), lambda i, j: (i, 0))),
        scratch_shapes=[pltpu.VMEM((bm, d), BF16)],
        compiler_params=_params(("arbitrary", "arbitrary"), vmem + SPILL_ROOM),
        name="norm_proj",
    )(*([x2] * X_PARTS), gain, wt, wt)


def _gla_kernel(*refs, heads, dk, dv, ts, n_cast):
    q_ref, k_ref, v_ref, og_ref, alr_ref, wup_ref, bup_ref, hg_ref = refs[:8]
    o_ref = refs[8 + n_cast]
    state_ref = refs[-1]
    _cast_slabs(refs[8:8 + n_cast], refs[9 + n_cast:9 + 2 * n_cast])

    @pl.when(pl.program_id(1) == 0)
    def _():
        state_ref[...] = jnp.zeros_like(state_ref)

    w_up = wup_ref[...]
    rank = w_up.shape[0]
    w_up = jnp.concatenate([w_up, jnp.zeros((LANES - rank, w_up.shape[1]), F32)], axis=0)
    z = _dot(alr_ref[...], w_up.astype(BF16)) + bup_ref[...]
    log_a = (jnp.minimum(z, 0.0) - jnp.log(1.0 + jnp.exp(-jnp.abs(z)))) * (1.0 / GLA_TAU)

    row = lax.broadcasted_iota(jnp.int32, (ts, ts), 0)
    col = lax.broadcasted_iota(jnp.int32, (ts, ts), 1)
    tri = ((row // CHUNK == col // CHUNK) & (col <= row)).astype(BF16)
    la_hi = log_a.astype(BF16)
    la_lo = (log_a - la_hi.astype(F32)).astype(BF16)
    b = _dot(tri, la_hi) + _dot(tri, la_lo)

    crow = lax.broadcasted_iota(jnp.int32, (CHUNK, CHUNK), 0)
    ccol = lax.broadcasted_iota(jnp.int32, (CHUNK, CHUNK), 1)
    lower = ccol <= crow

    scale = dk ** -0.5
    gain = hg_ref[...]
    states = [state_ref[h] for h in range(heads)]
    for c in range(ts // CHUNK):
        r = slice(c * CHUNK, (c + 1) * CHUNK)
        bc = b[r, :]
        mid = bc[CHUNK // 2:CHUNK // 2 + 1, :]
        last = bc[CHUNK - 1:CHUNK, :]
        e_fwd = jnp.exp(bc - mid)
        e_rev = jnp.exp(mid - bc)
        qc = q_ref[r, :].astype(F32) * scale
        kc = k_ref[r, :].astype(F32)
        qe = qc * e_fwd
        ke = kc * e_rev
        q_f = qe.astype(BF16)
        k_f = ke.astype(BF16)
        q_r = (qc * e_rev).astype(BF16)
        k_r = (kc * e_fwd).astype(BF16)
        k_l = (ke * jnp.exp(last - mid)).astype(BF16)
        q_i = (qe * jnp.exp(mid)).astype(BF16)
        decay = jnp.exp(last)
        for h in range(heads):
            ck = slice(h * dk, (h + 1) * dk)
            cv = slice(h * dv, (h + 1) * dv)
            vh = v_ref[r, cv]
            att = jnp.where(lower, _dot_nt(q_f[:, ck], k_f[:, ck]), _dot_nt(q_r[:, ck], k_r[:, ck]))
            av = _dot(jnp.concatenate([att.astype(BF16), k_l[:, ck].T], axis=0), vh)
            st = states[h]
            o = av[:CHUNK] + _dot(q_i[:, ck], st.astype(BF16))
            dcol = jnp.transpose(jnp.broadcast_to(decay[:, ck], (dk, dk)))
            states[h] = st * jnp.concatenate([dcol] * (dv // dk), axis=1) + av[CHUNK:]
            o = o * lax.rsqrt(jnp.mean(o * o, axis=-1, keepdims=True) + EPS) * gain
            og = og_ref[r, cv].astype(F32)
            o_ref[r, cv] = (o * (og * jax.nn.sigmoid(og))).astype(o_ref.dtype)
    for h in range(heads):
        state_ref[h] = states[h]


def _gla(proj, a_lr, w_up, b_up, head_gain, cast_weights, *, batch, seq, heads, dk, dv, cols, ts):
    m = proj.shape[0]
    ns = seq // ts
    dkt, dvt = heads * dk, heads * dv
    rank_pad = a_lr.shape[1]

    def rows(b, s):
        return b * ns + s

    c_in, c_out, c_shapes, c_vmem = _cast_plan(cast_weights, batch * ns, rows)
    kernel = functools.partial(_gla_kernel, heads=heads, dk=dk, dv=dv, ts=ts, n_cast=len(c_in))
    vmem = (2 * ts * (2 * dkt + 2 * dvt + rank_pad) * 2 + 2 * ts * dvt * 2 + heads * dk * dv * 4
            + c_vmem)
    return pl.pallas_call(
        kernel,
        out_shape=[jax.ShapeDtypeStruct((m, dvt), BF16)] + c_shapes,
        grid=(batch, ns),
        in_specs=[
            pl.BlockSpec((ts, dkt), lambda b, s: (rows(b, s), cols["q"] // dkt)),
            pl.BlockSpec((ts, dkt), lambda b, s: (rows(b, s), cols["k"] // dkt)),
            pl.BlockSpec((ts, dvt), lambda b, s: (rows(b, s), cols["v"] // dvt)),
            pl.BlockSpec((ts, dvt), lambda b, s: (rows(b, s), cols["og"] // dvt)),
            pl.BlockSpec((ts, rank_pad), lambda b, s: (rows(b, s), 0)),
            _resident(w_up.shape),
            _resident(b_up.shape),
            _resident(head_gain.shape),
        ] + c_in,
        out_specs=[pl.BlockSpec((ts, dvt), lambda b, s: (rows(b, s), 0))] + c_out,
        scratch_shapes=[pltpu.VMEM((heads, dk, dv), F32)],
        compiler_params=_params(("arbitrary", "arbitrary"), VMEM_LIMIT_CAP),
        name="gla",
    )(proj, proj, proj, proj, a_lr, w_up, b_up, head_gain, *cast_weights)


def _mix_out_kernel(*refs, tiles_per_seq, n_cast):
    (ax_ref, ab_ref, ac_ref, axp_ref, acp_ref, o_ref, ga_ref, gb_ref, x_ref,
     cw_ref, wa_ref, wb_ref, wm_ref, g_ref, gnext_ref) = refs[:15]
    out_ref, hnext_ref = refs[15 + n_cast:17 + n_cast]
    _cast_slabs(refs[15:15 + n_cast], refs[17 + n_cast:])

    u = ac_ref[...].astype(F32) * ax_ref[...].astype(F32)
    u_prev = acp_ref[...].astype(F32) * axp_ref[...].astype(F32)
    first = (pl.program_id(0) % tiles_per_seq) == 0
    u_prev = jnp.where(first, 0.0, u_prev)
    p1 = u_prev[SUBLANES - 1:SUBLANES, :]
    p2 = u_prev[SUBLANES - 2:SUBLANES - 1, :]
    row = lax.broadcasted_iota(jnp.int32, u.shape, 0)
    u1 = jnp.where(row == 0, p1, pltpu.roll(u, 1, 0))
    u2 = jnp.where(row == 0, p2, jnp.where(row == 1, p1, pltpu.roll(u, 2, 0)))
    y = cw_ref[0] * u2 + cw_ref[1] * u1 + cw_ref[2] * u
    ca = (ab_ref[...].astype(F32) * y).astype(BF16)

    ya = _dot(ca, wa_ref[...])
    yb = _dot(o_ref[...], wb_ref[...])
    mix = jax.nn.sigmoid(ga_ref[...].astype(F32)) * ya + jax.nn.sigmoid(gb_ref[...].astype(F32)) * yb
    m = _dot(mix.astype(BF16), wm_ref[...])
    x_new = x_ref[...] + _rms(m, g_ref[...])
    out_ref[...] = x_new
    hnext_ref[...] = _rms(x_new, gnext_ref[...]).astype(hnext_ref.dtype)


def _mix_out(proj, o, x2, conv_w, wa, wb, wm, gain, gain_next, cast_weights, *, seq, cols, bm):
    m, d = x2.shape
    ca = conv_w.shape[-1]
    cb = o.shape[1]
    tiles_per_seq = seq // bm
    halo = bm // SUBLANES

    def prev(i):
        return jnp.maximum(i * halo - 1, 0)

    c_in, c_out, c_shapes, c_vmem = _cast_plan(cast_weights, m // bm, lambda i: i)
    kernel = functools.partial(_mix_out_kernel, tiles_per_seq=tiles_per_seq, n_cast=len(c_in))
    vmem = (2 * bm * (3 * ca + cb + 3 * d) * 2 + 4 * bm * d * 4
            + (ca + cb + d) * d * 2 + 4 * bm * d * 4 + c_vmem)
    return pl.pallas_call(
        kernel,
        out_shape=[jax.ShapeDtypeStruct((m, d), F32), jax.ShapeDtypeStruct((m, d), BF16)] + c_shapes,
        grid=(m // bm,),
        in_specs=[
            pl.BlockSpec((bm, ca), lambda i: (i, cols["a_x"] // ca)),
            pl.BlockSpec((bm, ca), lambda i: (i, cols["a_b"] // ca)),
            pl.BlockSpec((bm, ca), lambda i: (i, cols["a_c"] // ca)),
            pl.BlockSpec((SUBLANES, ca), lambda i: (prev(i), cols["a_x"] // ca)),
            pl.BlockSpec((SUBLANES, ca), lambda i: (prev(i), cols["a_c"] // ca)),
            pl.BlockSpec((bm, cb), lambda i: (i, 0)),
            pl.BlockSpec((bm, d), lambda i: (i, cols["gate_a"] // d)),
            pl.BlockSpec((bm, d), lambda i: (i, cols["gate_b"] // d)),
            pl.BlockSpec((bm, d), lambda i: (i, 0)),
            _resident(conv_w.shape),
            _resident(wa.shape),
            _resident(wb.shape),
            _resident(wm.shape),
            _resident(gain.shape),
            _resident(gain_next.shape),
        ] + c_in,
        out_specs=[pl.BlockSpec((bm, d), lambda i: (i, 0))] * 2 + c_out,
        compiler_params=_params(("arbitrary",), vmem + SPILL_ROOM),
        name="mix_out",
    )(proj, proj, proj, proj, proj, o, proj, proj, x2, conv_w, wa, wb, wm, gain, gain_next,
      *cast_weights)


def _ffn_kernel(h_ref, wg_ref, wu_ref, wd_ref, gpost_ref, out_ref):
    j = pl.program_id(1)
    last = pl.num_programs(1) - 1

    def step(first_block, last_block):
        h = h_ref[...]
        g = _dot(h, wg_ref[...])
        u = _dot(h, wu_ref[...])
        a = (g * jax.nn.sigmoid(g) * u).astype(BF16)
        acc = _dot(a, wd_ref[...])
        if not first_block:
            acc = out_ref[...] + acc
        out_ref[...] = _rms(acc, gpost_ref[...]) if last_block else acc

    pl.when(j == 0)(lambda: step(True, False))
    pl.when((j > 0) & (j < last))(lambda: step(False, False))
    pl.when(j == last)(lambda: step(False, True))


def _ffn(h, wg, wu, wd, g_post, *, bm, bf):
    m, d = h.shape
    f = wg.shape[1]
    vmem = 2 * bm * d * 2 + 2 * bm * d * 4 + 2 * 3 * d * bf * 2 + 3 * bm * bf * 4
    return pl.pallas_call(
        _ffn_kernel,
        out_shape=jax.ShapeDtypeStruct((m, d), F32),
        grid=(m // bm, f // bf),
        in_specs=[
            pl.BlockSpec((bm, d), lambda i, j: (i, 0)),
            pl.BlockSpec((d, bf), lambda i, j: (0, j)),
            pl.BlockSpec((d, bf), lambda i, j: (0, j)),
            pl.BlockSpec((bf, d), lambda i, j: (j, 0)),
            pl.BlockSpec((1, d), lambda i, j: (0, 0)),
        ],
        out_specs=pl.BlockSpec((bm, d), lambda i, j: (i, 0)),
        compiler_params=_params(("parallel", "arbitrary"), vmem + SPILL_ROOM),
        name="ffn",
    )(h, wg, wu, wd, g_post)


def _ple_kernel(x_ref, f_ref, p_ref, gpre_ref, wpg_ref, wpp_ref, gpost_ref, out_ref):
    x = x_ref[...] + f_ref[...]
    h = _rms(x, gpre_ref[...]).astype(BF16)
    e = jax.nn.sigmoid(_dot(h, wpg_ref[...])) * _dot(p_ref[...].astype(BF16),
                                                      wpp_ref[...].astype(BF16))
    out_ref[...] = x + _rms(e, gpost_ref[...])


def _ple(x2, f2, p2, g_pre, wpg, wpp, g_post, *, bm):
    m, d = x2.shape
    pd = p2.shape[1]
    vmem = 6 * bm * d * 4 + 2 * bm * pd * 4 + d * d * 2 + pd * d * 4 + 4 * bm * d * 4
    return pl.pallas_call(
        _ple_kernel,
        out_shape=jax.ShapeDtypeStruct((m, d), F32),
        grid=(m // bm,),
        in_specs=[
            pl.BlockSpec((bm, d), lambda i: (i, 0)),
            pl.BlockSpec((bm, d), lambda i: (i, 0)),
            pl.BlockSpec((bm, pd), lambda i: (i, 0)),
            _resident(g_pre.shape),
            _resident(wpg.shape),
            _resident(wpp.shape),
            _resident(g_post.shape),
        ],
        out_specs=pl.BlockSpec((bm, d), lambda i: (i, 0)),
        compiler_params=_params(("parallel",), vmem + SPILL_ROOM),
        name="ple",
    )(x2, f2, p2, g_pre, wpg, wpp, g_post)


def _tiles(m, seq, n_proj, d_ff):
    tiles = dict(proj_rows=1024, proj_cols=1024, gla_rows=256, mix_rows=256,
                 ffn_rows=1024, ffn_cols=512, ple_rows=512)
    assert all(m % tiles[k] == 0 for k in ("proj_rows", "mix_rows", "ffn_rows", "ple_rows"))
    assert seq % tiles["gla_rows"] == 0 and tiles["gla_rows"] % CHUNK == 0
    assert seq % tiles["mix_rows"] == 0 and tiles["mix_rows"] % SUBLANES == 0
    assert n_proj % tiles["proj_cols"] == 0 and d_ff % tiles["ffn_cols"] == 0
    return tiles


def _proj_layout(mix_a, dk_total, mix_b, d):
    names = ["a_x", "a_b", "a_c", "q", "k", "v", "og", "gate_a", "gate_b"]
    widths = [mix_a, mix_a, mix_a, dk_total, dk_total, mix_b, mix_b, d, d]
    cols, off = {}, 0
    for nme, wdt in zip(names, widths):
        assert off % wdt == 0, (nme, off, wdt)
        cols[nme] = off
        off += wdt
    return cols, cols["gate_a"]


def kernel(x, p, w_in, conv_w, w_a_out, w_alpha_up, b_alpha_up, gla_head_gain, w_b_out, w_mix_out, g_pre_mix, g_post_mix, g_pre_ffn, g_post_ffn, w_ff_gate, w_ff_up, w_ff_down, g_pre_ple, g_post_ple, w_ple_gate, w_ple_proj):
    batch, seq, d = x.shape
    depth = w_in.shape[0]
    mix_a = conv_w.shape[-1]
    rank, dk_total = w_alpha_up.shape[1], w_alpha_up.shape[2]
    dv = gla_head_gain.shape[-1]
    mix_b = w_b_out.shape[1]
    heads = mix_b // dv
    dk = dk_total // heads
    assert rank <= LANES and seq % CHUNK == 0 and conv_w.shape[1] == CONV_WIDTH
    cols, lr0 = _proj_layout(mix_a, dk_total, mix_b, d)
    t = _tiles(batch * seq, seq, w_in.shape[2] - rank, w_ff_gate.shape[2])

    xs = x.reshape(batch * seq, d)
    for i in range(depth):
        proj, a_lr = _norm_proj(xs, g_pre_mix[i][None], w_in[i].T, lr0=lr0, rank=rank,
                                bm=t["proj_rows"], bn=t["proj_cols"])
        o, wa, wb, wm, wg = _gla(
            proj, a_lr, w_alpha_up[i], b_alpha_up[i][None], gla_head_gain[i][None],
            [w_a_out[i], w_b_out[i], w_mix_out[i], w_ff_gate[i]],
            batch=batch, seq=seq, heads=heads, dk=dk, dv=dv, cols=cols, ts=t["gla_rows"])
        xs, h_ffn, wu, wd, wpg = _mix_out(
            proj, o, xs, conv_w[i][:, None, :], wa, wb, wm, g_post_mix[i][None],
            g_pre_ffn[i][None],
            [w_ff_up[i], w_ff_down[i], w_ple_gate[i]], seq=seq, cols=cols, bm=t["mix_rows"])
        f_norm = _ffn(h_ffn, wg, wu, wd, g_post_ffn[i][None],
                      bm=t["ffn_rows"], bf=t["ffn_cols"])
        xs = _ple(xs, f_norm, p[i].reshape(batch * seq, -1), g_pre_ple[i][None], wpg,
                  w_ple_proj[i], g_post_ple[i][None], bm=t["ple_rows"])
    return xs.reshape(batch, seq, d)
```

```python
import functools

import jax
import jax.numpy as jnp
from jax import lax
from jax.experimental import pallas as pl
from jax.experimental.pallas import tpu as pltpu

EPS = 1e-6
CHUNK = 64
GLA_TAU = 16.0
CONV_WIDTH = 3
LANES = 128
SUBLANES = 8
VMEM_LIMIT_CAP = 60000 * 1024
SPILL_ROOM = 8 << 20

F32 = jnp.float32
BF16 = jnp.bfloat16


def _rms(x, gain):
    ms = jnp.mean(x * x, axis=-1, keepdims=True)
    return x * lax.rsqrt(ms + EPS) * gain


def _dot(a, b):
    return jnp.dot(a, b, preferred_element_type=F32)


def _dot_nt(a, b):
    return lax.dot_general(a, b, (((1,), (1,)), ((), ())), preferred_element_type=F32)


def _params(semantics, vmem_bytes):
    return pltpu.CompilerParams(
        dimension_semantics=semantics,
        vmem_limit_bytes=min(int(vmem_bytes), VMEM_LIMIT_CAP),
    )


def _resident(shape):
    return pl.BlockSpec(shape, lambda *_: (0,) * len(shape), pipeline_mode=pl.Buffered(1))


BF16_ROWS = 16


def _cast_plan(weights, n_slabs, slab_of):
    in_specs, out_specs, out_shapes, vmem = [], [], [], 0
    for w in weights:
        rows, width = w.shape
        assert rows % (n_slabs * BF16_ROWS) == 0, (w.shape, n_slabs)
        slab = (rows // n_slabs, width)
        spec = pl.BlockSpec(slab, lambda *idx: (slab_of(*idx), 0))
        in_specs.append(spec)
        out_specs.append(spec)
        out_shapes.append(jax.ShapeDtypeStruct(w.shape, BF16))
        vmem += 2 * slab[0] * slab[1] * (4 + 2)
    return in_specs, out_specs, out_shapes, vmem


def _cast_slabs(src_refs, dst_refs):
    for src, dst in zip(src_refs, dst_refs):
        dst[...] = src[...].astype(dst.dtype)


X_PARTS = 4


def _norm_proj_kernel(*refs):
    x_refs = refs[:X_PARTS]
    g_ref, wt_ref, wlr_ref, o_ref, lr_ref, h_ref = refs[X_PARTS:]

    @pl.when(pl.program_id(1) == 0)
    def _():
        d = h_ref.shape[1]
        w = d // X_PARTS
        ms = sum(jnp.sum(r[...] * r[...], axis=-1, keepdims=True) for r in x_refs) * (1.0 / d)
        rs = lax.rsqrt(ms + EPS)
        for k, r in enumerate(x_refs):
            c = slice(k * w, (k + 1) * w)
            h_ref[:, c] = (r[...] * rs * g_ref[:, c]).astype(BF16)
        wlr = wlr_ref[...]
        wlr = jnp.concatenate([wlr, jnp.zeros((LANES - wlr.shape[0], wlr.shape[1]), F32)], axis=0)
        lr_ref[...] = _dot_nt(h_ref[...], wlr.astype(BF16)).astype(lr_ref.dtype)

    o_ref[...] = _dot_nt(h_ref[...], wt_ref[...].astype(BF16)).astype(o_ref.dtype)


def _norm_proj(x2, gain, wt, *, lr0, rank, bm, bn):
    m, d = x2.shape
    n = wt.shape[0] - rank
    assert lr0 % bn == 0 and n % bn == 0 and rank % SUBLANES == 0

    def w_rows(i, j):
        skip = jnp.where(j * bn >= lr0, rank // SUBLANES, 0)
        return ((j * (bn // SUBLANES) + skip) * SUBLANES, 0)

    ni, nj = m // bm, n // bn
    switch = [(k + 1) * nj // (X_PARTS + 1) for k in range(X_PARTS)]
    assert d % (X_PARTS * LANES) == 0 and all(0 < s < nj for s in switch)

    def x_part(k):
        def index(i, j):
            return (jnp.minimum(i + jnp.where(j >= switch[k], 1, 0), ni - 1), k)
        return pl.BlockSpec((bm, d // X_PARTS), index)

    vmem = 2 * bm * d * 4 + 2 * bn * d * 4 + 2 * bm * bn * 2 + bm * d * 2 + bm * bn * 4 + d * bn * 2
    return pl.pallas_call(
        _norm_proj_kernel,
        out_shape=(jax.ShapeDtypeStruct((m, n), BF16), jax.ShapeDtypeStruct((m, LANES), BF16)),
        grid=(ni, nj),
        in_specs=[x_part(k) for k in range(X_PARTS)] + [
            pl.BlockSpec((1, d), lambda i, j: (0, 0)),
            pl.BlockSpec((pl.Element(bn), pl.Element(d)), w_rows),
            pl.BlockSpec((pl.Element(rank), pl.Element(d)), lambda i, j: (lr0, 0)),
        ],
        out_specs=(pl.BlockSpec((bm, bn), lambda i, j: (i, j)),
                   pl.BlockSpec((bm, LANES), lambda i, j: (i, 0))),
        scratch_shapes=[pltpu.VMEM((bm, d), BF16)],
        compiler_params=_params(("arbitrary", "arbitrary"), vmem + SPILL_ROOM),
        name="norm_proj",
    )(*([x2] * X_PARTS), gain, wt, wt)


def _gla_kernel(*refs, heads, dk, dv, ts, n_cast):
    q_ref, k_ref, v_ref, og_ref, alr_ref, wup_ref, bup_ref, hg_ref = refs[:8]
    o_ref = refs[8 + n_cast]
    state_ref = refs[-1]
    _cast_slabs(refs[8:8 + n_cast], refs[9 + n_cast:9 + 2 * n_cast])

    @pl.when(pl.program_id(1) == 0)
    def _():
        state_ref[...] = jnp.zeros_like(state_ref)

    w_up = wup_ref[...]
    rank = w_up.shape[0]
    w_up = jnp.concatenate([w_up, jnp.zeros((LANES - rank, w_up.shape[1]), F32)], axis=0)
    z = _dot(alr_ref[...], w_up.astype(BF16)) + bup_ref[...]
    log_a = (jnp.minimum(z, 0.0) - jnp.log(1.0 + jnp.exp(-jnp.abs(z)))) * (1.0 / GLA_TAU)

    row = lax.broadcasted_iota(jnp.int32, (ts, ts), 0)
    col = lax.broadcasted_iota(jnp.int32, (ts, ts), 1)
    tri = ((row // CHUNK == col // CHUNK) & (col <= row)).astype(BF16)
    la_hi = log_a.astype(BF16)
    la_lo = (log_a - la_hi.astype(F32)).astype(BF16)
    b = _dot(tri, la_hi) + _dot(tri, la_lo)

    crow = lax.broadcasted_iota(jnp.int32, (CHUNK, CHUNK), 0)
    ccol = lax.broadcasted_iota(jnp.int32, (CHUNK, CHUNK), 1)
    lower = ccol <= crow

    scale = dk ** -0.5
    gain = hg_ref[...]
    for c in range(ts // CHUNK):
        r = slice(c * CHUNK, (c + 1) * CHUNK)
        bc = b[r, :]
        mid = bc[CHUNK // 2:CHUNK // 2 + 1, :]
        last = bc[CHUNK - 1:CHUNK, :]
        e_fwd = jnp.exp(bc - mid)
        e_rev = jnp.exp(mid - bc)
        qc = q_ref[r, :].astype(F32) * scale
        kc = k_ref[r, :].astype(F32)
        qe = qc * e_fwd
        ke = kc * e_rev
        q_f = qe.astype(BF16)
        k_f = ke.astype(BF16)
        q_r = (qc * e_rev).astype(BF16)
        k_r = (kc * e_fwd).astype(BF16)
        k_l = (ke * jnp.exp(last - mid)).astype(BF16)
        q_i = (qe * jnp.exp(mid)).astype(BF16)
        decay = jnp.exp(last)
        for h in range(heads):
            ck = slice(h * dk, (h + 1) * dk)
            cv = slice(h * dv, (h + 1) * dv)
            vh = v_ref[r, cv]
            att = jnp.where(lower, _dot_nt(q_f[:, ck], k_f[:, ck]), _dot_nt(q_r[:, ck], k_r[:, ck]))
            av = _dot(jnp.concatenate([att.astype(BF16), k_l[:, ck].T], axis=0), vh)
            st = state_ref[h]
            o = av[:CHUNK] + _dot(q_i[:, ck], st.astype(BF16))
            dcol = jnp.transpose(jnp.broadcast_to(decay[:, ck], (dk, dk)))
            state_ref[h] = st * jnp.concatenate([dcol] * (dv // dk), axis=1) + av[CHUNK:]
            o = o * lax.rsqrt(jnp.mean(o * o, axis=-1, keepdims=True) + EPS) * gain
            og = og_ref[r, cv].astype(F32)
            o_ref[r, cv] = (o * (og * jax.nn.sigmoid(og))).astype(o_ref.dtype)


def _gla(proj, a_lr, w_up, b_up, head_gain, cast_weights, *, batch, seq, heads, dk, dv, cols, ts):
    m = proj.shape[0]
    ns = seq // ts
    dkt, dvt = heads * dk, heads * dv
    rank_pad = a_lr.shape[1]

    def rows(b, s):
        return b * ns + s

    c_in, c_out, c_shapes, c_vmem = _cast_plan(cast_weights, batch * ns, rows)
    kernel = functools.partial(_gla_kernel, heads=heads, dk=dk, dv=dv, ts=ts, n_cast=len(c_in))
    vmem = (2 * ts * (2 * dkt + 2 * dvt + rank_pad) * 2 + 2 * ts * dvt * 2 + heads * dk * dv * 4
            + c_vmem)
    return pl.pallas_call(
        kernel,
        out_shape=[jax.ShapeDtypeStruct((m, dvt), BF16)] + c_shapes,
        grid=(batch, ns),
        in_specs=[
            pl.BlockSpec((ts, dkt), lambda b, s: (rows(b, s), cols["q"] // dkt)),
            pl.BlockSpec((ts, dkt), lambda b, s: (rows(b, s), cols["k"] // dkt)),
            pl.BlockSpec((ts, dvt), lambda b, s: (rows(b, s), cols["v"] // dvt)),
            pl.BlockSpec((ts, dvt), lambda b, s: (rows(b, s), cols["og"] // dvt)),
            pl.BlockSpec((ts, rank_pad), lambda b, s: (rows(b, s), 0)),
            _resident(w_up.shape),
            _resident(b_up.shape),
            _resident(head_gain.shape),
        ] + c_in,
        out_specs=[pl.BlockSpec((ts, dvt), lambda b, s: (rows(b, s), 0))] + c_out,
        scratch_shapes=[pltpu.VMEM((heads, dk, dv), F32)],
        compiler_params=_params(("arbitrary", "arbitrary"), VMEM_LIMIT_CAP),
        name="gla",
    )(proj, proj, proj, proj, a_lr, w_up, b_up, head_gain, *cast_weights)


def _mix_out_kernel(*refs, tiles_per_seq, n_cast):
    (ax_ref, ab_ref, ac_ref, axp_ref, acp_ref, o_ref, ga_ref, gb_ref, x_ref,
     cw_ref, wa_ref, wb_ref, wm_ref, g_ref, gnext_ref) = refs[:15]
    out_ref, hnext_ref = refs[15 + n_cast:17 + n_cast]
    _cast_slabs(refs[15:15 + n_cast], refs[17 + n_cast:])

    u = ac_ref[...].astype(F32) * ax_ref[...].astype(F32)
    u_prev = acp_ref[...].astype(F32) * axp_ref[...].astype(F32)
    first = (pl.program_id(0) % tiles_per_seq) == 0
    u_prev = jnp.where(first, 0.0, u_prev)
    p1 = u_prev[SUBLANES - 1:SUBLANES, :]
    p2 = u_prev[SUBLANES - 2:SUBLANES - 1, :]
    row = lax.broadcasted_iota(jnp.int32, u.shape, 0)
    u1 = jnp.where(row == 0, p1, pltpu.roll(u, 1, 0))
    u2 = jnp.where(row == 0, p2, jnp.where(row == 1, p1, pltpu.roll(u, 2, 0)))
    y = cw_ref[0] * u2 + cw_ref[1] * u1 + cw_ref[2] * u
    ca = (ab_ref[...].astype(F32) * y).astype(BF16)

    ya = _dot(ca, wa_ref[...])
    yb = _dot(o_ref[...], wb_ref[...])
    mix = jax.nn.sigmoid(ga_ref[...].astype(F32)) * ya + jax.nn.sigmoid(gb_ref[...].astype(F32)) * yb
    m = _dot(mix.astype(BF16), wm_ref[...])
    x_new = x_ref[...] + _rms(m, g_ref[...])
    out_ref[...] = x_new
    hnext_ref[...] = _rms(x_new, gnext_ref[...]).astype(hnext_ref.dtype)


def _mix_out(proj, o, x2, conv_w, wa, wb, wm, gain, gain_next, cast_weights, *, seq, cols, bm):
    m, d = x2.shape
    ca = conv_w.shape[-1]
    cb = o.shape[1]
    tiles_per_seq = seq // bm
    halo = bm // SUBLANES

    def prev(i):
        return jnp.maximum(i * halo - 1, 0)

    c_in, c_out, c_shapes, c_vmem = _cast_plan(cast_weights, m // bm, lambda i: i)
    kernel = functools.partial(_mix_out_kernel, tiles_per_seq=tiles_per_seq, n_cast=len(c_in))
    vmem = (2 * bm * (3 * ca + cb + 3 * d) * 2 + 4 * bm * d * 4
            + (ca + cb + d) * d * 2 + 4 * bm * d * 4 + c_vmem)
    return pl.pallas_call(
        kernel,
        out_shape=[jax.ShapeDtypeStruct((m, d), F32), jax.ShapeDtypeStruct((m, d), BF16)] + c_shapes,
        grid=(m // bm,),
        in_specs=[
            pl.BlockSpec((bm, ca), lambda i: (i, cols["a_x"] // ca)),
            pl.BlockSpec((bm, ca), lambda i: (i, cols["a_b"] // ca)),
            pl.BlockSpec((bm, ca), lambda i: (i, cols["a_c"] // ca)),
            pl.BlockSpec((SUBLANES, ca), lambda i: (prev(i), cols["a_x"] // ca)),
            pl.BlockSpec((SUBLANES, ca), lambda i: (prev(i), cols["a_c"] // ca)),
            pl.BlockSpec((bm, cb), lambda i: (i, 0)),
            pl.BlockSpec((bm, d), lambda i: (i, cols["gate_a"] // d)),
            pl.BlockSpec((bm, d), lambda i: (i, cols["gate_b"] // d)),
            pl.BlockSpec((bm, d), lambda i: (i, 0)),
            _resident(conv_w.shape),
            _resident(wa.shape),
            _resident(wb.shape),
            _resident(wm.shape),
            _resident(gain.shape),
            _resident(gain_next.shape),
        ] + c_in,
        out_specs=[pl.BlockSpec((bm, d), lambda i: (i, 0))] * 2 + c_out,
        compiler_params=_params(("arbitrary",), vmem + SPILL_ROOM),
        name="mix_out",
    )(proj, proj, proj, proj, proj, o, proj, proj, x2, conv_w, wa, wb, wm, gain, gain_next,
      *cast_weights)


def _ffn_kernel(h_ref, wg_ref, wu_ref, wd_ref, gpost_ref, out_ref):
    j = pl.program_id(1)

    @pl.when(j == 0)
    def _():
        out_ref[...] = jnp.zeros_like(out_ref)

    h = h_ref[...]
    g = _dot(h, wg_ref[...])
    u = _dot(h, wu_ref[...])
    a = (g * jax.nn.sigmoid(g) * u).astype(BF16)
    out_ref[...] += _dot(a, wd_ref[...])

    @pl.when(j == pl.num_programs(1) - 1)
    def _():
        out_ref[...] = _rms(out_ref[...], gpost_ref[...])


def _ffn(h, wg, wu, wd, g_post, *, bm, bf):
    m, d = h.shape
    f = wg.shape[1]
    vmem = 2 * bm * d * 2 + 2 * bm * d * 4 + 2 * 3 * d * bf * 2 + 3 * bm * bf * 4
    return pl.pallas_call(
        _ffn_kernel,
        out_shape=jax.ShapeDtypeStruct((m, d), F32),
        grid=(m // bm, f // bf),
        in_specs=[
            pl.BlockSpec((bm, d), lambda i, j: (i, 0)),
            pl.BlockSpec((d, bf), lambda i, j: (0, j)),
            pl.BlockSpec((d, bf), lambda i, j: (0, j)),
            pl.BlockSpec((bf, d), lambda i, j: (j, 0)),
            pl.BlockSpec((1, d), lambda i, j: (0, 0)),
        ],
        out_specs=pl.BlockSpec((bm, d), lambda i, j: (i, 0)),
        compiler_params=_params(("parallel", "arbitrary"), vmem + SPILL_ROOM),
        name="ffn",
    )(h, wg, wu, wd, g_post)


def _ple_kernel(x_ref, f_ref, p_ref, gpre_ref, wpg_ref, wpp_ref, gpost_ref, out_ref):
    x = x_ref[...] + f_ref[...]
    h = _rms(x, gpre_ref[...]).astype(BF16)
    e = jax.nn.sigmoid(_dot(h, wpg_ref[...])) * _dot(p_ref[...].astype(BF16),
                                                      wpp_ref[...].astype(BF16))
    out_ref[...] = x + _rms(e, gpost_ref[...])


def _ple(x2, f2, p2, g_pre, wpg, wpp, g_post, *, bm):
    m, d = x2.shape
    pd = p2.shape[1]
    vmem = 6 * bm * d * 4 + 2 * bm * pd * 4 + d * d * 2 + pd * d * 4 + 4 * bm * d * 4
    return pl.pallas_call(
        _ple_kernel,
        out_shape=jax.ShapeDtypeStruct((m, d), F32),
        grid=(m // bm,),
        in_specs=[
            pl.BlockSpec((bm, d), lambda i: (i, 0)),
            pl.BlockSpec((bm, d), lambda i: (i, 0)),
            pl.BlockSpec((bm, pd), lambda i: (i, 0)),
            _resident(g_pre.shape),
            _resident(wpg.shape),
            _resident(wpp.shape),
            _resident(g_post.shape),
        ],
        out_specs=pl.BlockSpec((bm, d), lambda i: (i, 0)),
        compiler_params=_params(("parallel",), vmem + SPILL_ROOM),
        name="ple",
    )(x2, f2, p2, g_pre, wpg, wpp, g_post)


def _tiles(m, seq, n_proj, d_ff):
    tiles = dict(proj_rows=1024, proj_cols=1024, gla_rows=256, mix_rows=256,
                 ffn_rows=1024, ffn_cols=512, ple_rows=512)
    assert all(m % tiles[k] == 0 for k in ("proj_rows", "mix_rows", "ffn_rows", "ple_rows"))
    assert seq % tiles["gla_rows"] == 0 and tiles["gla_rows"] % CHUNK == 0
    assert seq % tiles["mix_rows"] == 0 and tiles["mix_rows"] % SUBLANES == 0
    assert n_proj % tiles["proj_cols"] == 0 and d_ff % tiles["ffn_cols"] == 0
    return tiles


def _proj_layout(mix_a, dk_total, mix_b, d):
    names = ["a_x", "a_b", "a_c", "q", "k", "v", "og", "gate_a", "gate_b"]
    widths = [mix_a, mix_a, mix_a, dk_total, dk_total, mix_b, mix_b, d, d]
    cols, off = {}, 0
    for nme, wdt in zip(names, widths):
        assert off % wdt == 0, (nme, off, wdt)
        cols[nme] = off
        off += wdt
    return cols, cols["gate_a"]


def kernel(x, p, w_in, conv_w, w_a_out, w_alpha_up, b_alpha_up, gla_head_gain, w_b_out, w_mix_out, g_pre_mix, g_post_mix, g_pre_ffn, g_post_ffn, w_ff_gate, w_ff_up, w_ff_down, g_pre_ple, g_post_ple, w_ple_gate, w_ple_proj):
    batch, seq, d = x.shape
    depth = w_in.shape[0]
    mix_a = conv_w.shape[-1]
    rank, dk_total = w_alpha_up.shape[1], w_alpha_up.shape[2]
    dv = gla_head_gain.shape[-1]
    mix_b = w_b_out.shape[1]
    heads = mix_b // dv
    dk = dk_total // heads
    assert rank <= LANES and seq % CHUNK == 0 and conv_w.shape[1] == CONV_WIDTH
    cols, lr0 = _proj_layout(mix_a, dk_total, mix_b, d)
    t = _tiles(batch * seq, seq, w_in.shape[2] - rank, w_ff_gate.shape[2])

    xs = x.reshape(batch * seq, d)
    for i in range(depth):
        proj, a_lr = _norm_proj(xs, g_pre_mix[i][None], w_in[i].T, lr0=lr0, rank=rank,
                                bm=t["proj_rows"], bn=t["proj_cols"])
        o, wa, wb, wm, wg = _gla(
            proj, a_lr, w_alpha_up[i], b_alpha_up[i][None], gla_head_gain[i][None],
            [w_a_out[i], w_b_out[i], w_mix_out[i], w_ff_gate[i]],
            batch=batch, seq=seq, heads=heads, dk=dk, dv=dv, cols=cols, ts=t["gla_rows"])
        xs, h_ffn, wu, wd, wpg = _mix_out(
            proj, o, xs, conv_w[i][:, None, :], wa, wb, wm, g_post_mix[i][None],
            g_pre_ffn[i][None],
            [w_ff_up[i], w_ff_down[i], w_ple_gate[i]], seq=seq, cols=cols, bm=t["mix_rows"])
        f_norm = _ffn(h_ffn, wg, wu, wd, g_post_ffn[i][None],
                      bm=t["ffn_rows"], bf=t["ffn_cols"])
        xs = _ple(xs, f_norm, p[i].reshape(batch * seq, -1), g_pre_ple[i][None], wpg,
                  w_ple_proj[i], g_post_ple[i][None], bm=t["ple_rows"])
    return xs.reshape(batch, seq, d)
```

```python
import functools

import jax
import jax.numpy as jnp
from jax import lax
from jax.experimental import pallas as pl
from jax.experimental.pallas import tpu as pltpu

EPS = 1e-6
CHUNK = 64
GLA_TAU = 16.0
CONV_WIDTH = 3
LANES = 128
SUBLANES = 8
VMEM_LIMIT_CAP = 60000 * 1024
SPILL_ROOM = 8 << 20

F32 = jnp.float32
BF16 = jnp.bfloat16


def _rms(x, gain):
    ms = jnp.mean(x * x, axis=-1, keepdims=True)
    return x * lax.rsqrt(ms + EPS) * gain


def _dot(a, b):
    return jnp.dot(a, b, preferred_element_type=F32)


def _dot_nt(a, b):
    return lax.dot_general(a, b, (((1,), (1,)), ((), ())), preferred_element_type=F32)


def _params(semantics, vmem_bytes):
    return pltpu.CompilerParams(
        dimension_semantics=semantics,
        vmem_limit_bytes=min(int(vmem_bytes), VMEM_LIMIT_CAP),
    )


def _resident(shape):
    return pl.BlockSpec(shape, lambda *_: (0,) * len(shape), pipeline_mode=pl.Buffered(1))


BF16_ROWS = 16


def _cast_plan(weights, n_slabs, slab_of):
    in_specs, out_specs, out_shapes, vmem = [], [], [], 0
    for w in weights:
        rows, width = w.shape
        assert rows % (n_slabs * BF16_ROWS) == 0, (w.shape, n_slabs)
        slab = (rows // n_slabs, width)
        spec = pl.BlockSpec(slab, lambda *idx: (slab_of(*idx), 0))
        in_specs.append(spec)
        out_specs.append(spec)
        out_shapes.append(jax.ShapeDtypeStruct(w.shape, BF16))
        vmem += 2 * slab[0] * slab[1] * (4 + 2)
    return in_specs, out_specs, out_shapes, vmem


def _cast_slabs(src_refs, dst_refs):
    for src, dst in zip(src_refs, dst_refs):
        dst[...] = src[...].astype(dst.dtype)


X_PARTS = 4


def _norm_proj_kernel(*refs):
    x_refs = refs[:X_PARTS]
    g_ref, wt_ref, wlr_ref, o_ref, lr_ref, h_ref = refs[X_PARTS:]

    @pl.when(pl.program_id(1) == 0)
    def _():
        rows = h_ref.shape[0] // X_PARTS
        for k, r in enumerate(x_refs):
            h_ref[k * rows:(k + 1) * rows, :] = _rms(r[...], g_ref[...]).astype(BF16)
        wlr = wlr_ref[...]
        wlr = jnp.concatenate([wlr, jnp.zeros((LANES - wlr.shape[0], wlr.shape[1]), F32)], axis=0)
        lr_ref[...] = _dot_nt(h_ref[...], wlr.astype(BF16)).astype(lr_ref.dtype)

    o_ref[...] = _dot_nt(h_ref[...], wt_ref[...].astype(BF16)).astype(o_ref.dtype)


def _norm_proj(x2, gain, wt, *, lr0, rank, bm, bn):
    m, d = x2.shape
    n = wt.shape[0] - rank
    assert lr0 % bn == 0 and n % bn == 0 and rank % SUBLANES == 0

    def w_rows(i, j):
        skip = jnp.where(j * bn >= lr0, rank // SUBLANES, 0)
        return ((j * (bn // SUBLANES) + skip) * SUBLANES, 0)

    ni, nj = m // bm, n // bn
    switch = [(k + 1) * nj // (X_PARTS + 1) for k in range(X_PARTS)]
    assert bm % (X_PARTS * BF16_ROWS) == 0 and all(0 < s < nj for s in switch)

    def x_part(k):
        def index(i, j):
            tile = jnp.minimum(i + jnp.where(j >= switch[k], 1, 0), ni - 1)
            return (tile * X_PARTS + k, 0)
        return pl.BlockSpec((bm // X_PARTS, d), index)

    vmem = 2 * bm * d * 4 + 2 * bn * d * 4 + 2 * bm * bn * 2 + bm * d * 2 + bm * bn * 4 + d * bn * 2
    return pl.pallas_call(
        _norm_proj_kernel,
        out_shape=(jax.ShapeDtypeStruct((m, n), BF16), jax.ShapeDtypeStruct((m, LANES), BF16)),
        grid=(ni, nj),
        in_specs=[x_part(k) for k in range(X_PARTS)] + [
            pl.BlockSpec((1, d), lambda i, j: (0, 0)),
            pl.BlockSpec((pl.Element(bn), pl.Element(d)), w_rows),
            pl.BlockSpec((pl.Element(rank), pl.Element(d)), lambda i, j: (lr0, 0)),
        ],
        out_specs=(pl.BlockSpec((bm, bn), lambda i, j: (i, j)),
                   pl.BlockSpec((bm, LANES), lambda i, j: (i, 0))),
        scratch_shapes=[pltpu.VMEM((bm, d), BF16)],
        compiler_params=_params(("arbitrary", "arbitrary"), vmem + SPILL_ROOM),
        name="norm_proj",
    )(*([x2] * X_PARTS), gain, wt, wt)


def _gla_kernel(*refs, heads, dk, dv, ts, n_cast):
    q_ref, k_ref, v_ref, og_ref, alr_ref, wup_ref, bup_ref, hg_ref = refs[:8]
    o_ref = refs[8 + n_cast]
    state_ref = refs[-1]
    _cast_slabs(refs[8:8 + n_cast], refs[9 + n_cast:9 + 2 * n_cast])

    @pl.when(pl.program_id(1) == 0)
    def _():
        state_ref[...] = jnp.zeros_like(state_ref)

    w_up = wup_ref[...]
    rank = w_up.shape[0]
    w_up = jnp.concatenate([w_up, jnp.zeros((LANES - rank, w_up.shape[1]), F32)], axis=0)
    z = _dot(alr_ref[...], w_up.astype(BF16)) + bup_ref[...]
    log_a = (jnp.minimum(z, 0.0) - jnp.log(1.0 + jnp.exp(-jnp.abs(z)))) * (1.0 / GLA_TAU)

    row = lax.broadcasted_iota(jnp.int32, (ts, ts), 0)
    col = lax.broadcasted_iota(jnp.int32, (ts, ts), 1)
    tri = ((row // CHUNK == col // CHUNK) & (col <= row)).astype(BF16)
    la_hi = log_a.astype(BF16)
    la_lo = (log_a - la_hi.astype(F32)).astype(BF16)
    b = _dot(tri, la_hi) + _dot(tri, la_lo)

    crow = lax.broadcasted_iota(jnp.int32, (CHUNK, CHUNK), 0)
    ccol = lax.broadcasted_iota(jnp.int32, (CHUNK, CHUNK), 1)
    lower = ccol <= crow

    scale = dk ** -0.5
    gain = hg_ref[...]
    for c in range(ts // CHUNK):
        r = slice(c * CHUNK, (c + 1) * CHUNK)
        bc = b[r, :]
        mid = bc[CHUNK // 2:CHUNK // 2 + 1, :]
        last = bc[CHUNK - 1:CHUNK, :]
        e_fwd = jnp.exp(bc - mid)
        e_rev = jnp.exp(mid - bc)
        qc = q_ref[r, :].astype(F32) * scale
        kc = k_ref[r, :].astype(F32)
        qe = qc * e_fwd
        ke = kc * e_rev
        q_f = qe.astype(BF16)
        k_f = ke.astype(BF16)
        q_r = (qc * e_rev).astype(BF16)
        k_r = (kc * e_fwd).astype(BF16)
        k_l = (ke * jnp.exp(last - mid)).astype(BF16)
        q_i = (qe * jnp.exp(mid)).astype(BF16)
        decay = jnp.exp(last)
        for h in range(heads):
            ck = slice(h * dk, (h + 1) * dk)
            cv = slice(h * dv, (h + 1) * dv)
            vh = v_ref[r, cv]
            att = jnp.where(lower, _dot_nt(q_f[:, ck], k_f[:, ck]), _dot_nt(q_r[:, ck], k_r[:, ck]))
            av = _dot(jnp.concatenate([att.astype(BF16), k_l[:, ck].T], axis=0), vh)
            st = state_ref[h]
            o = av[:CHUNK] + _dot(q_i[:, ck], st.astype(BF16))
            dcol = jnp.transpose(jnp.broadcast_to(decay[:, ck], (dk, dk)))
            state_ref[h] = st * jnp.concatenate([dcol] * (dv // dk), axis=1) + av[CHUNK:]
            o = o * lax.rsqrt(jnp.mean(o * o, axis=-1, keepdims=True) + EPS) * gain
            og = og_ref[r, cv].astype(F32)
            o_ref[r, cv] = (o * (og * jax.nn.sigmoid(og))).astype(o_ref.dtype)


def _gla(proj, a_lr, w_up, b_up, head_gain, cast_weights, *, batch, seq, heads, dk, dv, cols, ts):
    m = proj.shape[0]
    ns = seq // ts
    dkt, dvt = heads * dk, heads * dv
    rank_pad = a_lr.shape[1]

    def rows(b, s):
        return b * ns + s

    c_in, c_out, c_shapes, c_vmem = _cast_plan(cast_weights, batch * ns, rows)
    kernel = functools.partial(_gla_kernel, heads=heads, dk=dk, dv=dv, ts=ts, n_cast=len(c_in))
    vmem = (2 * ts * (2 * dkt + 2 * dvt + rank_pad) * 2 + 2 * ts * dvt * 2 + heads * dk * dv * 4
            + c_vmem)
    return pl.pallas_call(
        kernel,
        out_shape=[jax.ShapeDtypeStruct((m, dvt), BF16)] + c_shapes,
        grid=(batch, ns),
        in_specs=[
            pl.BlockSpec((ts, dkt), lambda b, s: (rows(b, s), cols["q"] // dkt)),
            pl.BlockSpec((ts, dkt), lambda b, s: (rows(b, s), cols["k"] // dkt)),
            pl.BlockSpec((ts, dvt), lambda b, s: (rows(b, s), cols["v"] // dvt)),
            pl.BlockSpec((ts, dvt), lambda b, s: (rows(b, s), cols["og"] // dvt)),
            pl.BlockSpec((ts, rank_pad), lambda b, s: (rows(b, s), 0)),
            _resident(w_up.shape),
            _resident(b_up.shape),
            _resident(head_gain.shape),
        ] + c_in,
        out_specs=[pl.BlockSpec((ts, dvt), lambda b, s: (rows(b, s), 0))] + c_out,
        scratch_shapes=[pltpu.VMEM((heads, dk, dv), F32)],
        compiler_params=_params(("arbitrary", "arbitrary"), VMEM_LIMIT_CAP),
        name="gla",
    )(proj, proj, proj, proj, a_lr, w_up, b_up, head_gain, *cast_weights)


def _mix_out_kernel(*refs, tiles_per_seq, n_cast):
    (ax_ref, ab_ref, ac_ref, axp_ref, acp_ref, o_ref, ga_ref, gb_ref, x_ref,
     cw_ref, wa_ref, wb_ref, wm_ref, g_ref, gnext_ref) = refs[:15]
    out_ref, hnext_ref = refs[15 + n_cast:17 + n_cast]
    _cast_slabs(refs[15:15 + n_cast], refs[17 + n_cast:])

    u = ac_ref[...].astype(F32) * ax_ref[...].astype(F32)
    u_prev = acp_ref[...].astype(F32) * axp_ref[...].astype(F32)
    first = (pl.program_id(0) % tiles_per_seq) == 0
    u_prev = jnp.where(first, 0.0, u_prev)
    p1 = u_prev[SUBLANES - 1:SUBLANES, :]
    p2 = u_prev[SUBLANES - 2:SUBLANES - 1, :]
    row = lax.broadcasted_iota(jnp.int32, u.shape, 0)
    u1 = jnp.where(row == 0, p1, pltpu.roll(u, 1, 0))
    u2 = jnp.where(row == 0, p2, jnp.where(row == 1, p1, pltpu.roll(u, 2, 0)))
    y = cw_ref[0] * u2 + cw_ref[1] * u1 + cw_ref[2] * u
    ca = (ab_ref[...].astype(F32) * y).astype(BF16)

    ya = _dot(ca, wa_ref[...])
    yb = _dot(o_ref[...], wb_ref[...])
    mix = jax.nn.sigmoid(ga_ref[...].astype(F32)) * ya + jax.nn.sigmoid(gb_ref[...].astype(F32)) * yb
    m = _dot(mix.astype(BF16), wm_ref[...])
    x_new = x_ref[...] + _rms(m, g_ref[...])
    out_ref[...] = x_new
    hnext_ref[...] = _rms(x_new, gnext_ref[...]).astype(hnext_ref.dtype)


def _mix_out(proj, o, x2, conv_w, wa, wb, wm, gain, gain_next, cast_weights, *, seq, cols, bm):
    m, d = x2.shape
    ca = conv_w.shape[-1]
    cb = o.shape[1]
    tiles_per_seq = seq // bm
    halo = bm // SUBLANES

    def prev(i):
        return jnp.maximum(i * halo - 1, 0)

    c_in, c_out, c_shapes, c_vmem = _cast_plan(cast_weights, m // bm, lambda i: i)
    kernel = functools.partial(_mix_out_kernel, tiles_per_seq=tiles_per_seq, n_cast=len(c_in))
    vmem = (2 * bm * (3 * ca + cb + 3 * d) * 2 + 4 * bm * d * 4
            + (ca + cb + d) * d * 2 + 4 * bm * d * 4 + c_vmem)
    return pl.pallas_call(
        kernel,
        out_shape=[jax.ShapeDtypeStruct((m, d), F32), jax.ShapeDtypeStruct((m, d), BF16)] + c_shapes,
        grid=(m // bm,),
        in_specs=[
            pl.BlockSpec((bm, ca), lambda i: (i, cols["a_x"] // ca)),
            pl.BlockSpec((bm, ca), lambda i: (i, cols["a_b"] // ca)),
            pl.BlockSpec((bm, ca), lambda i: (i, cols["a_c"] // ca)),
            pl.BlockSpec((SUBLANES, ca), lambda i: (prev(i), cols["a_x"] // ca)),
            pl.BlockSpec((SUBLANES, ca), lambda i: (prev(i), cols["a_c"] // ca)),
            pl.BlockSpec((bm, cb), lambda i: (i, 0)),
            pl.BlockSpec((bm, d), lambda i: (i, cols["gate_a"] // d)),
            pl.BlockSpec((bm, d), lambda i: (i, cols["gate_b"] // d)),
            pl.BlockSpec((bm, d), lambda i: (i, 0)),
            _resident(conv_w.shape),
            _resident(wa.shape),
            _resident(wb.shape),
            _resident(wm.shape),
            _resident(gain.shape),
            _resident(gain_next.shape),
        ] + c_in,
        out_specs=[pl.BlockSpec((bm, d), lambda i: (i, 0))] * 2 + c_out,
        compiler_params=_params(("arbitrary",), vmem + SPILL_ROOM),
        name="mix_out",
    )(proj, proj, proj, proj, proj, o, proj, proj, x2, conv_w, wa, wb, wm, gain, gain_next,
      *cast_weights)


def _ffn_kernel(h_ref, wg_ref, wu_ref, wd_ref, gpost_ref, out_ref):
    j = pl.program_id(1)

    @pl.when(j == 0)
    def _():
        out_ref[...] = jnp.zeros_like(out_ref)

    h = h_ref[...]
    g = _dot(h, wg_ref[...])
    u = _dot(h, wu_ref[...])
    a = (g * jax.nn.sigmoid(g) * u).astype(BF16)
    out_ref[...] += _dot(a, wd_ref[...])

    @pl.when(j == pl.num_programs(1) - 1)
    def _():
        out_ref[...] = _rms(out_ref[...], gpost_ref[...])


def _ffn(h, wg, wu, wd, g_post, *, bm, bf):
    m, d = h.shape
    f = wg.shape[1]
    vmem = 2 * bm * d * 2 + 2 * bm * d * 4 + 2 * 3 * d * bf * 2 + 3 * bm * bf * 4
    return pl.pallas_call(
        _ffn_kernel,
        out_shape=jax.ShapeDtypeStruct((m, d), F32),
        grid=(m // bm, f // bf),
        in_specs=[
            pl.BlockSpec((bm, d), lambda i, j: (i, 0)),
            pl.BlockSpec((d, bf), lambda i, j: (0, j)),
            pl.BlockSpec((d, bf), lambda i, j: (0, j)),
            pl.BlockSpec((bf, d), lambda i, j: (j, 0)),
            pl.BlockSpec((1, d), lambda i, j: (0, 0)),
        ],
        out_specs=pl.BlockSpec((bm, d), lambda i, j: (i, 0)),
        compiler_params=_params(("parallel", "arbitrary"), vmem + SPILL_ROOM),
        name="ffn",
    )(h, wg, wu, wd, g_post)


def _ple_kernel(x_ref, f_ref, p_ref, gpre_ref, wpg_ref, wpp_ref, gpost_ref, out_ref):
    x = x_ref[...] + f_ref[...]
    h = _rms(x, gpre_ref[...]).astype(BF16)
    e = jax.nn.sigmoid(_dot(h, wpg_ref[...])) * _dot(p_ref[...].astype(BF16),
                                                      wpp_ref[...].astype(BF16))
    out_ref[...] = x + _rms(e, gpost_ref[...])


def _ple(x2, f2, p2, g_pre, wpg, wpp, g_post, *, bm):
    m, d = x2.shape
    pd = p2.shape[1]
    vmem = 6 * bm * d * 4 + 2 * bm * pd * 4 + d * d * 2 + pd * d * 4 + 4 * bm * d * 4
    return pl.pallas_call(
        _ple_kernel,
        out_shape=jax.ShapeDtypeStruct((m, d), F32),
        grid=(m // bm,),
        in_specs=[
            pl.BlockSpec((bm, d), lambda i: (i, 0)),
            pl.BlockSpec((bm, d), lambda i: (i, 0)),
            pl.BlockSpec((bm, pd), lambda i: (i, 0)),
            _resident(g_pre.shape),
            _resident(wpg.shape),
            _resident(wpp.shape),
            _resident(g_post.shape),
        ],
        out_specs=pl.BlockSpec((bm, d), lambda i: (i, 0)),
        compiler_params=_params(("parallel",), vmem + SPILL_ROOM),
        name="ple",
    )(x2, f2, p2, g_pre, wpg, wpp, g_post)


def _tiles(m, seq, n_proj, d_ff):
    tiles = dict(proj_rows=1024, proj_cols=1024, gla_rows=256, mix_rows=256,
                 ffn_rows=1024, ffn_cols=512, ple_rows=512)
    assert all(m % tiles[k] == 0 for k in ("proj_rows", "mix_rows", "ffn_rows", "ple_rows"))
    assert seq % tiles["gla_rows"] == 0 and tiles["gla_rows"] % CHUNK == 0
    assert seq % tiles["mix_rows"] == 0 and tiles["mix_rows"] % SUBLANES == 0
    assert n_proj % tiles["proj_cols"] == 0 and d_ff % tiles["ffn_cols"] == 0
    return tiles


def _proj_layout(mix_a, dk_total, mix_b, d):
    names = ["a_x", "a_b", "a_c", "q", "k", "v", "og", "gate_a", "gate_b"]
    widths = [mix_a, mix_a, mix_a, dk_total, dk_total, mix_b, mix_b, d, d]
    cols, off = {}, 0
    for nme, wdt in zip(names, widths):
        assert off % wdt == 0, (nme, off, wdt)
        cols[nme] = off
        off += wdt
    return cols, cols["gate_a"]


def kernel(x, p, w_in, conv_w, w_a_out, w_alpha_up, b_alpha_up, gla_head_gain, w_b_out, w_mix_out, g_pre_mix, g_post_mix, g_pre_ffn, g_post_ffn, w_ff_gate, w_ff_up, w_ff_down, g_pre_ple, g_post_ple, w_ple_gate, w_ple_proj):
    batch, seq, d = x.shape
    depth = w_in.shape[0]
    mix_a = conv_w.shape[-1]
    rank, dk_total = w_alpha_up.shape[1], w_alpha_up.shape[2]
    dv = gla_head_gain.shape[-1]
    mix_b = w_b_out.shape[1]
    heads = mix_b // dv
    dk = dk_total // heads
    assert rank <= LANES and seq % CHUNK == 0 and conv_w.shape[1] == CONV_WIDTH
    cols, lr0 = _proj_layout(mix_a, dk_total, mix_b, d)
    t = _tiles(batch * seq, seq, w_in.shape[2] - rank, w_ff_gate.shape[2])

    xs = x.reshape(batch * seq, d)
    for i in range(depth):
        proj, a_lr = _norm_proj(xs, g_pre_mix[i][None], w_in[i].T, lr0=lr0, rank=rank,
                                bm=t["proj_rows"], bn=t["proj_cols"])
        o, wa, wb, wm, wg, wu = _gla(
            proj, a_lr, w_alpha_up[i], b_alpha_up[i][None], gla_head_gain[i][None],
            [w_a_out[i], w_b_out[i], w_mix_out[i], w_ff_gate[i], w_ff_up[i]],
            batch=batch, seq=seq, heads=heads, dk=dk, dv=dv, cols=cols, ts=t["gla_rows"])
        xs, h_ffn, wd, wpg = _mix_out(
            proj, o, xs, conv_w[i][:, None, :], wa, wb, wm, g_post_mix[i][None],
            g_pre_ffn[i][None],
            [w_ff_down[i], w_ple_gate[i]], seq=seq, cols=cols, bm=t["mix_rows"])
        f_norm = _ffn(h_ffn, wg, wu, wd, g_post_ffn[i][None],
                      bm=t["ffn_rows"], bf=t["ffn_cols"])
        xs = _ple(xs, f_norm, p[i].reshape(batch * seq, -1), g_pre_ple[i][None], wpg,
                  w_ple_proj[i], g_post_ple[i][None], bm=t["ple_rows"])
    return xs.reshape(batch, seq, d)
```

```python
import functools

import jax
import jax.numpy as jnp
from jax import lax
from jax.experimental import pallas as pl
from jax.experimental.pallas import tpu as pltpu

EPS = 1e-6
CHUNK = 64
GLA_TAU = 16.0
CONV_WIDTH = 3
LANES = 128
SUBLANES = 8
VMEM_LIMIT_CAP = 60000 * 1024
SPILL_ROOM = 8 << 20

F32 = jnp.float32
BF16 = jnp.bfloat16


def _rms(x, gain):
    ms = jnp.mean(x * x, axis=-1, keepdims=True)
    return x * lax.rsqrt(ms + EPS) * gain


def _dot(a, b):
    return jnp.dot(a, b, preferred_element_type=F32)


def _dot_nt(a, b):
    return lax.dot_general(a, b, (((1,), (1,)), ((), ())), preferred_element_type=F32)


def _params(semantics, vmem_bytes):
    return pltpu.CompilerParams(
        dimension_semantics=semantics,
        vmem_limit_bytes=min(int(vmem_bytes), VMEM_LIMIT_CAP),
    )


def _resident(shape):
    return pl.BlockSpec(shape, lambda *_: (0,) * len(shape), pipeline_mode=pl.Buffered(1))


BF16_ROWS = 16


def _cast_plan(weights, n_slabs, slab_of):
    in_specs, out_specs, out_shapes, vmem = [], [], [], 0
    for w in weights:
        rows, width = w.shape
        assert rows % (n_slabs * BF16_ROWS) == 0, (w.shape, n_slabs)
        slab = (rows // n_slabs, width)
        spec = pl.BlockSpec(slab, lambda *idx: (slab_of(*idx), 0))
        in_specs.append(spec)
        out_specs.append(spec)
        out_shapes.append(jax.ShapeDtypeStruct(w.shape, BF16))
        vmem += 2 * slab[0] * slab[1] * (4 + 2)
    return in_specs, out_specs, out_shapes, vmem


def _cast_slabs(src_refs, dst_refs):
    for src, dst in zip(src_refs, dst_refs):
        dst[...] = src[...].astype(dst.dtype)


X_PARTS = 4


def _norm_proj_kernel(*refs):
    x_refs = refs[:X_PARTS]
    g_ref, wt_ref, wlr_ref, o_ref, lr_ref, h_ref = refs[X_PARTS:]

    @pl.when(pl.program_id(1) == 0)
    def _():
        rows = h_ref.shape[0] // X_PARTS
        for k, r in enumerate(x_refs):
            h_ref[k * rows:(k + 1) * rows, :] = _rms(r[...], g_ref[...]).astype(BF16)
        wlr = wlr_ref[...]
        wlr = jnp.concatenate([wlr, jnp.zeros((LANES - wlr.shape[0], wlr.shape[1]), F32)], axis=0)
        lr_ref[...] = _dot_nt(h_ref[...], wlr.astype(BF16)).astype(lr_ref.dtype)

    o_ref[...] = _dot_nt(h_ref[...], wt_ref[...].astype(BF16)).astype(o_ref.dtype)


def _norm_proj(x2, gain, wt, *, lr0, rank, bm, bn):
    m, d = x2.shape
    n = wt.shape[0] - rank
    assert lr0 % bn == 0 and n % bn == 0 and rank % SUBLANES == 0

    def w_rows(i, j):
        skip = jnp.where(j * bn >= lr0, rank // SUBLANES, 0)
        return ((j * (bn // SUBLANES) + skip) * SUBLANES, 0)

    ni, nj = m // bm, n // bn
    switch = [(k + 1) * nj // (X_PARTS + 1) for k in range(X_PARTS)]
    assert bm % (X_PARTS * BF16_ROWS) == 0 and all(0 < s < nj for s in switch)

    def x_part(k):
        def index(i, j):
            tile = jnp.minimum(i + jnp.where(j >= switch[k], 1, 0), ni - 1)
            return (tile * X_PARTS + k, 0)
        return pl.BlockSpec((bm // X_PARTS, d), index)

    vmem = 2 * bm * d * 4 + 2 * bn * d * 4 + 2 * bm * bn * 2 + bm * d * 2 + bm * bn * 4 + d * bn * 2
    return pl.pallas_call(
        _norm_proj_kernel,
        out_shape=(jax.ShapeDtypeStruct((m, n), BF16), jax.ShapeDtypeStruct((m, LANES), BF16)),
        grid=(ni, nj),
        in_specs=[x_part(k) for k in range(X_PARTS)] + [
            pl.BlockSpec((1, d), lambda i, j: (0, 0)),
            pl.BlockSpec((pl.Element(bn), pl.Element(d)), w_rows),
            pl.BlockSpec((pl.Element(rank), pl.Element(d)), lambda i, j: (lr0, 0)),
        ],
        out_specs=(pl.BlockSpec((bm, bn), lambda i, j: (i, j)),
                   pl.BlockSpec((bm, LANES), lambda i, j: (i, 0))),
        scratch_shapes=[pltpu.VMEM((bm, d), BF16)],
        compiler_params=_params(("arbitrary", "arbitrary"), vmem + SPILL_ROOM),
        name="norm_proj",
    )(*([x2] * X_PARTS), gain, wt, wt)


def _gla_kernel(*refs, heads, dk, dv, ts, n_cast):
    q_ref, k_ref, v_ref, og_ref, alr_ref, wup_ref, bup_ref, hg_ref = refs[:8]
    o_ref = refs[8 + n_cast]
    state_ref = refs[-1]
    _cast_slabs(refs[8:8 + n_cast], refs[9 + n_cast:9 + 2 * n_cast])

    @pl.when(pl.program_id(1) == 0)
    def _():
        state_ref[...] = jnp.zeros_like(state_ref)

    w_up = wup_ref[...]
    rank = w_up.shape[0]
    w_up = jnp.concatenate([w_up, jnp.zeros((LANES - rank, w_up.shape[1]), F32)], axis=0)
    z = _dot(alr_ref[...], w_up.astype(BF16)) + bup_ref[...]
    log_a = (jnp.minimum(z, 0.0) - jnp.log(1.0 + jnp.exp(-jnp.abs(z)))) * (1.0 / GLA_TAU)

    row = lax.broadcasted_iota(jnp.int32, (ts, ts), 0)
    col = lax.broadcasted_iota(jnp.int32, (ts, ts), 1)
    tri = ((row // CHUNK == col // CHUNK) & (col <= row)).astype(BF16)
    la_hi = log_a.astype(BF16)
    la_lo = (log_a - la_hi.astype(F32)).astype(BF16)
    b = _dot(tri, la_hi) + _dot(tri, la_lo)

    crow = lax.broadcasted_iota(jnp.int32, (CHUNK, CHUNK), 0)
    ccol = lax.broadcasted_iota(jnp.int32, (CHUNK, CHUNK), 1)
    lower = ccol <= crow

    scale = dk ** -0.5
    gain = hg_ref[...]
    for c in range(ts // CHUNK):
        r = slice(c * CHUNK, (c + 1) * CHUNK)
        bc = b[r, :]
        mid = bc[CHUNK // 2:CHUNK // 2 + 1, :]
        last = bc[CHUNK - 1:CHUNK, :]
        e_fwd = jnp.exp(bc - mid)
        e_rev = jnp.exp(mid - bc)
        qc = q_ref[r, :].astype(F32) * scale
        kc = k_ref[r, :].astype(F32)
        qe = qc * e_fwd
        ke = kc * e_rev
        q_f = qe.astype(BF16)
        k_f = ke.astype(BF16)
        q_r = (qc * e_rev).astype(BF16)
        k_r = (kc * e_fwd).astype(BF16)
        k_l = (ke * jnp.exp(last - mid)).astype(BF16)
        q_i = (qe * jnp.exp(mid)).astype(BF16)
        decay = jnp.exp(last)
        for h in range(heads):
            ck = slice(h * dk, (h + 1) * dk)
            cv = slice(h * dv, (h + 1) * dv)
            vh = v_ref[r, cv]
            att = jnp.where(lower, _dot_nt(q_f[:, ck], k_f[:, ck]), _dot_nt(q_r[:, ck], k_r[:, ck]))
            av = _dot(jnp.concatenate([att.astype(BF16), k_l[:, ck].T], axis=0), vh)
            st = state_ref[h]
            o = av[:CHUNK] + _dot(q_i[:, ck], st.astype(BF16))
            dcol = jnp.transpose(jnp.broadcast_to(decay[:, ck], (dk, dk)))
            state_ref[h] = st * jnp.concatenate([dcol] * (dv // dk), axis=1) + av[CHUNK:]
            o = o * lax.rsqrt(jnp.mean(o * o, axis=-1, keepdims=True) + EPS) * gain
            og = og_ref[r, cv].astype(F32)
            o_ref[r, cv] = (o * (og * jax.nn.sigmoid(og))).astype(o_ref.dtype)


def _gla(proj, a_lr, w_up, b_up, head_gain, cast_weights, *, batch, seq, heads, dk, dv, cols, ts):
    m = proj.shape[0]
    ns = seq // ts
    dkt, dvt = heads * dk, heads * dv
    rank_pad = a_lr.shape[1]

    def rows(b, s):
        return b * ns + s

    c_in, c_out, c_shapes, c_vmem = _cast_plan(cast_weights, batch * ns, rows)
    kernel = functools.partial(_gla_kernel, heads=heads, dk=dk, dv=dv, ts=ts, n_cast=len(c_in))
    vmem = (2 * ts * (2 * dkt + 2 * dvt + rank_pad) * 2 + 2 * ts * dvt * 2 + heads * dk * dv * 4
            + c_vmem)
    return pl.pallas_call(
        kernel,
        out_shape=[jax.ShapeDtypeStruct((m, dvt), BF16)] + c_shapes,
        grid=(batch, ns),
        in_specs=[
            pl.BlockSpec((ts, dkt), lambda b, s: (rows(b, s), cols["q"] // dkt)),
            pl.BlockSpec((ts, dkt), lambda b, s: (rows(b, s), cols["k"] // dkt)),
            pl.BlockSpec((ts, dvt), lambda b, s: (rows(b, s), cols["v"] // dvt)),
            pl.BlockSpec((ts, dvt), lambda b, s: (rows(b, s), cols["og"] // dvt)),
            pl.BlockSpec((ts, rank_pad), lambda b, s: (rows(b, s), 0)),
            _resident(w_up.shape),
            _resident(b_up.shape),
            _resident(head_gain.shape),
        ] + c_in,
        out_specs=[pl.BlockSpec((ts, dvt), lambda b, s: (rows(b, s), 0))] + c_out,
        scratch_shapes=[pltpu.VMEM((heads, dk, dv), F32)],
        compiler_params=_params(("arbitrary", "arbitrary"), VMEM_LIMIT_CAP),
        name="gla",
    )(proj, proj, proj, proj, a_lr, w_up, b_up, head_gain, *cast_weights)


def _mix_out_kernel(*refs, tiles_per_seq, n_cast):
    (a_ref, ap_ref, o_ref, ga_ref, gb_ref, x_ref,
     cw_ref, wa_ref, wb_ref, wm_ref, g_ref, gnext_ref) = refs[:12]
    out_ref, hnext_ref = refs[12 + n_cast:14 + n_cast]
    _cast_slabs(refs[12:12 + n_cast], refs[14 + n_cast:])

    nc = wa_ref.shape[0]
    u = a_ref[:, 2 * nc:].astype(F32) * a_ref[:, :nc].astype(F32)
    u_prev = ap_ref[:, 2 * nc:].astype(F32) * ap_ref[:, :nc].astype(F32)
    first = (pl.program_id(0) % tiles_per_seq) == 0
    u_prev = jnp.where(first, 0.0, u_prev)
    p1 = u_prev[SUBLANES - 1:SUBLANES, :]
    p2 = u_prev[SUBLANES - 2:SUBLANES - 1, :]
    row = lax.broadcasted_iota(jnp.int32, u.shape, 0)
    u1 = jnp.where(row == 0, p1, pltpu.roll(u, 1, 0))
    u2 = jnp.where(row == 0, p2, jnp.where(row == 1, p1, pltpu.roll(u, 2, 0)))
    y = cw_ref[0] * u2 + cw_ref[1] * u1 + cw_ref[2] * u
    ca = (a_ref[:, nc:2 * nc].astype(F32) * y).astype(BF16)

    ya = _dot(ca, wa_ref[...])
    yb = _dot(o_ref[...], wb_ref[...])
    mix = jax.nn.sigmoid(ga_ref[...].astype(F32)) * ya + jax.nn.sigmoid(gb_ref[...].astype(F32)) * yb
    m = _dot(mix.astype(BF16), wm_ref[...])
    x_new = x_ref[...] + _rms(m, g_ref[...])
    out_ref[...] = x_new
    hnext_ref[...] = _rms(x_new, gnext_ref[...]).astype(hnext_ref.dtype)


def _mix_out(proj, o, x2, conv_w, wa, wb, wm, gain, gain_next, cast_weights, *, seq, cols, bm):
    m, d = x2.shape
    ca = conv_w.shape[-1]
    cb = o.shape[1]
    tiles_per_seq = seq // bm
    halo = bm // SUBLANES
    assert cols["a_b"] == cols["a_x"] + ca and cols["a_c"] == cols["a_x"] + 2 * ca
    assert cols["a_x"] % (3 * ca) == 0

    def prev(i):
        return jnp.maximum(i * halo - 1, 0)

    c_in, c_out, c_shapes, c_vmem = _cast_plan(cast_weights, m // bm, lambda i: i)
    kernel = functools.partial(_mix_out_kernel, tiles_per_seq=tiles_per_seq, n_cast=len(c_in))
    vmem = (2 * bm * (3 * ca + cb + 3 * d) * 2 + 4 * bm * d * 4
            + (ca + cb + d) * d * 2 + 4 * bm * d * 4 + c_vmem)
    return pl.pallas_call(
        kernel,
        out_shape=[jax.ShapeDtypeStruct((m, d), F32), jax.ShapeDtypeStruct((m, d), BF16)] + c_shapes,
        grid=(m // bm,),
        in_specs=[
            pl.BlockSpec((bm, 3 * ca), lambda i: (i, cols["a_x"] // (3 * ca))),
            pl.BlockSpec((SUBLANES, 3 * ca), lambda i: (prev(i), cols["a_x"] // (3 * ca))),
            pl.BlockSpec((bm, cb), lambda i: (i, 0)),
            pl.BlockSpec((bm, d), lambda i: (i, cols["gate_a"] // d)),
            pl.BlockSpec((bm, d), lambda i: (i, cols["gate_b"] // d)),
            pl.BlockSpec((bm, d), lambda i: (i, 0)),
            _resident(conv_w.shape),
            _resident(wa.shape),
            _resident(wb.shape),
            _resident(wm.shape),
            _resident(gain.shape),
            _resident(gain_next.shape),
        ] + c_in,
        out_specs=[pl.BlockSpec((bm, d), lambda i: (i, 0))] * 2 + c_out,
        compiler_params=_params(("arbitrary",), vmem + SPILL_ROOM),
        name="mix_out",
    )(proj, proj, o, proj, proj, x2, conv_w, wa, wb, wm, gain, gain_next, *cast_weights)


def _ffn_kernel(h_ref, wg_ref, wu_ref, wd_ref, gpost_ref, out_ref):
    j = pl.program_id(1)

    @pl.when(j == 0)
    def _():
        out_ref[...] = jnp.zeros_like(out_ref)

    h = h_ref[...]
    g = _dot(h, wg_ref[...])
    u = _dot(h, wu_ref[...])
    a = (g * jax.nn.sigmoid(g) * u).astype(BF16)
    out_ref[...] += _dot(a, wd_ref[...])

    @pl.when(j == pl.num_programs(1) - 1)
    def _():
        out_ref[...] = _rms(out_ref[...], gpost_ref[...])


def _ffn(h, wg, wu, wd, g_post, *, bm, bf):
    m, d = h.shape
    f = wg.shape[1]
    vmem = 2 * bm * d * 2 + 2 * bm * d * 4 + 2 * 3 * d * bf * 2 + 3 * bm * bf * 4
    return pl.pallas_call(
        _ffn_kernel,
        out_shape=jax.ShapeDtypeStruct((m, d), F32),
        grid=(m // bm, f // bf),
        in_specs=[
            pl.BlockSpec((bm, d), lambda i, j: (i, 0)),
            pl.BlockSpec((d, bf), lambda i, j: (0, j)),
            pl.BlockSpec((d, bf), lambda i, j: (0, j)),
            pl.BlockSpec((bf, d), lambda i, j: (j, 0)),
            pl.BlockSpec((1, d), lambda i, j: (0, 0)),
        ],
        out_specs=pl.BlockSpec((bm, d), lambda i, j: (i, 0)),
        compiler_params=_params(("parallel", "arbitrary"), vmem + SPILL_ROOM),
        name="ffn",
    )(h, wg, wu, wd, g_post)


def _ple_kernel(x_ref, f_ref, p_ref, gpre_ref, wpg_ref, wpp_ref, gpost_ref, out_ref):
    x = x_ref[...] + f_ref[...]
    h = _rms(x, gpre_ref[...]).astype(BF16)
    e = jax.nn.sigmoid(_dot(h, wpg_ref[...])) * _dot(p_ref[...].astype(BF16),
                                                      wpp_ref[...].astype(BF16))
    out_ref[...] = x + _rms(e, gpost_ref[...])


def _ple(x2, f2, p2, g_pre, wpg, wpp, g_post, *, bm):
    m, d = x2.shape
    pd = p2.shape[1]
    vmem = 6 * bm * d * 4 + 2 * bm * pd * 4 + d * d * 2 + pd * d * 4 + 4 * bm * d * 4
    return pl.pallas_call(
        _ple_kernel,
        out_shape=jax.ShapeDtypeStruct((m, d), F32),
        grid=(m // bm,),
        in_specs=[
            pl.BlockSpec((bm, d), lambda i: (i, 0)),
            pl.BlockSpec((bm, d), lambda i: (i, 0)),
            pl.BlockSpec((bm, pd), lambda i: (i, 0)),
            _resident(g_pre.shape),
            _resident(wpg.shape),
            _resident(wpp.shape),
            _resident(g_post.shape),
        ],
        out_specs=pl.BlockSpec((bm, d), lambda i: (i, 0)),
        compiler_params=_params(("parallel",), vmem + SPILL_ROOM),
        name="ple",
    )(x2, f2, p2, g_pre, wpg, wpp, g_post)


def _tiles(m, seq, n_proj, d_ff):
    tiles = dict(proj_rows=1024, proj_cols=1024, gla_rows=256, mix_rows=256,
                 ffn_rows=1024, ffn_cols=512, ple_rows=512)
    assert all(m % tiles[k] == 0 for k in ("proj_rows", "mix_rows", "ffn_rows", "ple_rows"))
    assert seq % tiles["gla_rows"] == 0 and tiles["gla_rows"] % CHUNK == 0
    assert seq % tiles["mix_rows"] == 0 and tiles["mix_rows"] % SUBLANES == 0
    assert n_proj % tiles["proj_cols"] == 0 and d_ff % tiles["ffn_cols"] == 0
    return tiles


def _proj_layout(mix_a, dk_total, mix_b, d):
    names = ["a_x", "a_b", "a_c", "q", "k", "v", "og", "gate_a", "gate_b"]
    widths = [mix_a, mix_a, mix_a, dk_total, dk_total, mix_b, mix_b, d, d]
    cols, off = {}, 0
    for nme, wdt in zip(names, widths):
        assert off % wdt == 0, (nme, off, wdt)
        cols[nme] = off
        off += wdt
    return cols, cols["gate_a"]


def kernel(x, p, w_in, conv_w, w_a_out, w_alpha_up, b_alpha_up, gla_head_gain, w_b_out, w_mix_out, g_pre_mix, g_post_mix, g_pre_ffn, g_post_ffn, w_ff_gate, w_ff_up, w_ff_down, g_pre_ple, g_post_ple, w_ple_gate, w_ple_proj):
    batch, seq, d = x.shape
    depth = w_in.shape[0]
    mix_a = conv_w.shape[-1]
    rank, dk_total = w_alpha_up.shape[1], w_alpha_up.shape[2]
    dv = gla_head_gain.shape[-1]
    mix_b = w_b_out.shape[1]
    heads = mix_b // dv
    dk = dk_total // heads
    assert rank <= LANES and seq % CHUNK == 0 and conv_w.shape[1] == CONV_WIDTH
    cols, lr0 = _proj_layout(mix_a, dk_total, mix_b, d)
    t = _tiles(batch * seq, seq, w_in.shape[2] - rank, w_ff_gate.shape[2])

    xs = x.reshape(batch * seq, d)
    for i in range(depth):
        proj, a_lr = _norm_proj(xs, g_pre_mix[i][None], w_in[i].T, lr0=lr0, rank=rank,
                                bm=t["proj_rows"], bn=t["proj_cols"])
        o, wa, wb, wm, wg = _gla(
            proj, a_lr, w_alpha_up[i], b_alpha_up[i][None], gla_head_gain[i][None],
            [w_a_out[i], w_b_out[i], w_mix_out[i], w_ff_gate[i]],
            batch=batch, seq=seq, heads=heads, dk=dk, dv=dv, cols=cols, ts=t["gla_rows"])
        xs, h_ffn, wu, wd, wpg = _mix_out(
            proj, o, xs, conv_w[i][:, None, :], wa, wb, wm, g_post_mix[i][None],
            g_pre_ffn[i][None],
            [w_ff_up[i], w_ff_down[i], w_ple_gate[i]], seq=seq, cols=cols, bm=t["mix_rows"])
        f_norm = _ffn(h_ffn, wg, wu, wd, g_post_ffn[i][None],
                      bm=t["ffn_rows"], bf=t["ffn_cols"])
        xs = _ple(xs, f_norm, p[i].reshape(batch * seq, -1), g_pre_ple[i][None], wpg,
                  w_ple_proj[i], g_post_ple[i][None], bm=t["ple_rows"])
    return xs.reshape(batch, seq, d)
```

```python
import functools

import jax
import jax.numpy as jnp
from jax import lax
from jax.experimental import pallas as pl
from jax.experimental.pallas import tpu as pltpu

EPS = 1e-6
CHUNK = 64
GLA_TAU = 16.0
CONV_WIDTH = 3
LANES = 128
SUBLANES = 8
VMEM_LIMIT_CAP = 60000 * 1024
SPILL_ROOM = 8 << 20

F32 = jnp.float32
BF16 = jnp.bfloat16


def _rms(x, gain):
    ms = jnp.mean(x * x, axis=-1, keepdims=True)
    return x * lax.rsqrt(ms + EPS) * gain


def _dot(a, b):
    return jnp.dot(a, b, preferred_element_type=F32)


def _dot_nt(a, b):
    return lax.dot_general(a, b, (((1,), (1,)), ((), ())), preferred_element_type=F32)


def _params(semantics, vmem_bytes):
    return pltpu.CompilerParams(
        dimension_semantics=semantics,
        vmem_limit_bytes=min(int(vmem_bytes), VMEM_LIMIT_CAP),
    )


def _resident(shape):
    return pl.BlockSpec(shape, lambda *_: (0,) * len(shape), pipeline_mode=pl.Buffered(1))


BF16_ROWS = 16


def _cast_plan(weights, n_slabs, slab_of):
    in_specs, out_specs, out_shapes, vmem = [], [], [], 0
    for w in weights:
        rows, width = w.shape
        assert rows % (n_slabs * BF16_ROWS) == 0, (w.shape, n_slabs)
        slab = (rows // n_slabs, width)
        spec = pl.BlockSpec(slab, lambda *idx: (slab_of(*idx), 0))
        in_specs.append(spec)
        out_specs.append(spec)
        out_shapes.append(jax.ShapeDtypeStruct(w.shape, BF16))
        vmem += 2 * slab[0] * slab[1] * (4 + 2)
    return in_specs, out_specs, out_shapes, vmem


def _cast_slabs(src_refs, dst_refs):
    for src, dst in zip(src_refs, dst_refs):
        dst[...] = src[...].astype(dst.dtype)


X_PARTS = 4


def _norm_proj_kernel(*refs, first):
    x_refs = refs[:X_PARTS]
    g_ref, wt_ref, wlr_ref = refs[X_PARTS:X_PARTS + 3]
    h_ref = refs[-1]
    if first:
        o_ref, lr_ref, wcopy_ref = refs[X_PARTS + 3:X_PARTS + 6]
    else:
        o_ref, lr_ref = refs[X_PARTS + 5:X_PARTS + 7]

    @pl.when(pl.program_id(1) == 0)
    def _():
        rows = h_ref.shape[0] // X_PARTS
        for k, r in enumerate(x_refs):
            h_ref[k * rows:(k + 1) * rows, :] = _rms(r[...], g_ref[...]).astype(BF16)
        wlr = wlr_ref[...]
        wlr = jnp.concatenate([wlr, jnp.zeros((LANES - wlr.shape[0], wlr.shape[1]), F32)], axis=0)
        lr_ref[...] = _dot_nt(h_ref[...], wlr.astype(BF16)).astype(lr_ref.dtype)

    w = wt_ref[...].astype(BF16)
    if first:
        wcopy_ref[...] = w
    o_ref[...] = _dot_nt(h_ref[...], w).astype(o_ref.dtype)


def _norm_proj(x2, gain, wt, *, lr0, rank, bm, bn):
    m, d = x2.shape
    n = wt.shape[0] - rank
    assert lr0 % bn == 0 and n % bn == 0 and rank % SUBLANES == 0
    ni, nj = m // bm, n // bn
    switch = [(k + 1) * nj // (X_PARTS + 1) for k in range(X_PARTS)]
    assert bm % (X_PARTS * BF16_ROWS) == 0 and all(0 < s < nj for s in switch) and ni > 1

    def w_rows(i, j):
        skip = jnp.where(j * bn >= lr0, rank // SUBLANES, 0)
        return ((j * (bn // SUBLANES) + skip) * SUBLANES, 0)

    def call(first, w_arg, carried):
        i0, tiles = (0, 1) if first else (1, ni - 1)

        def x_part(k):
            def index(i, j):
                tile = jnp.minimum(i0 + i + jnp.where(j >= switch[k], 1, 0), i0 + tiles - 1)
                return (tile * X_PARTS + k, 0)
            return pl.BlockSpec((bm // X_PARTS, d), index)

        if first:
            w_spec = pl.BlockSpec((pl.Element(bn), pl.Element(d)), w_rows)
        else:
            w_spec = pl.BlockSpec((bn, d), lambda i, j: (j, 0))
        out_shape = [jax.ShapeDtypeStruct((m, n), BF16), jax.ShapeDtypeStruct((m, LANES), BF16)]
        out_specs = [pl.BlockSpec((bm, bn), lambda i, j: (i0 + i, j)),
                     pl.BlockSpec((bm, LANES), lambda i, j: (i0 + i, 0))]
        if first:
            out_shape.append(jax.ShapeDtypeStruct((n, d), BF16))
            out_specs.append(pl.BlockSpec((bn, d), lambda i, j: (j, 0)))
        n_in = X_PARTS + 3
        vmem = (2 * bm * d * 4 + 2 * bn * d * 4 + 2 * bm * bn * 2 + bm * d * 2 + bm * bn * 4
                + 3 * d * bn * 2)
        return pl.pallas_call(
            functools.partial(_norm_proj_kernel, first=first),
            out_shape=out_shape,
            grid=(tiles, nj),
            in_specs=[x_part(k) for k in range(X_PARTS)] + [
                pl.BlockSpec((1, d), lambda i, j: (0, 0)),
                w_spec,
                pl.BlockSpec((pl.Element(rank), pl.Element(d)), lambda i, j: (lr0, 0)),
            ] + [pl.BlockSpec(memory_space=pl.ANY)] * len(carried),
            out_specs=out_specs,
            input_output_aliases={n_in + k: k for k in range(len(carried))},
            scratch_shapes=[pltpu.VMEM((bm, d), BF16)],
            compiler_params=_params(("arbitrary", "arbitrary"), vmem + SPILL_ROOM),
            name="norm_proj_first" if first else "norm_proj_rest",
        )(*([x2] * X_PARTS), gain, w_arg, wt, *carried)

    proj, a_lr, w_bf16 = call(True, wt, [])
    return call(False, w_bf16, [proj, a_lr])


def _gla_kernel(*refs, heads, dk, dv, ts, n_cast):
    q_ref, k_ref, v_ref, og_ref, alr_ref, wup_ref, bup_ref, hg_ref = refs[:8]
    o_ref = refs[8 + n_cast]
    state_ref = refs[-1]
    _cast_slabs(refs[8:8 + n_cast], refs[9 + n_cast:9 + 2 * n_cast])

    @pl.when(pl.program_id(1) == 0)
    def _():
        state_ref[...] = jnp.zeros_like(state_ref)

    w_up = wup_ref[...]
    rank = w_up.shape[0]
    w_up = jnp.concatenate([w_up, jnp.zeros((LANES - rank, w_up.shape[1]), F32)], axis=0)
    z = _dot(alr_ref[...], w_up.astype(BF16)) + bup_ref[...]
    log_a = (jnp.minimum(z, 0.0) - jnp.log(1.0 + jnp.exp(-jnp.abs(z)))) * (1.0 / GLA_TAU)

    row = lax.broadcasted_iota(jnp.int32, (ts, ts), 0)
    col = lax.broadcasted_iota(jnp.int32, (ts, ts), 1)
    tri = ((row // CHUNK == col // CHUNK) & (col <= row)).astype(BF16)
    la_hi = log_a.astype(BF16)
    la_lo = (log_a - la_hi.astype(F32)).astype(BF16)
    b = _dot(tri, la_hi) + _dot(tri, la_lo)

    crow = lax.broadcasted_iota(jnp.int32, (CHUNK, CHUNK), 0)
    ccol = lax.broadcasted_iota(jnp.int32, (CHUNK, CHUNK), 1)
    lower = ccol <= crow

    scale = dk ** -0.5
    gain = hg_ref[...]
    for c in range(ts // CHUNK):
        r = slice(c * CHUNK, (c + 1) * CHUNK)
        bc = b[r, :]
        mid = bc[CHUNK // 2:CHUNK // 2 + 1, :]
        last = bc[CHUNK - 1:CHUNK, :]
        e_fwd = jnp.exp(bc - mid)
        e_rev = jnp.exp(mid - bc)
        qc = q_ref[r, :].astype(F32) * scale
        kc = k_ref[r, :].astype(F32)
        qe = qc * e_fwd
        ke = kc * e_rev
        q_f = qe.astype(BF16)
        k_f = ke.astype(BF16)
        q_r = (qc * e_rev).astype(BF16)
        k_r = (kc * e_fwd).astype(BF16)
        k_l = (ke * jnp.exp(last - mid)).astype(BF16)
        q_i = (qe * jnp.exp(mid)).astype(BF16)
        decay = jnp.exp(last)
        for h in range(heads):
            ck = slice(h * dk, (h + 1) * dk)
            cv = slice(h * dv, (h + 1) * dv)
            vh = v_ref[r, cv]
            att = jnp.where(lower, _dot_nt(q_f[:, ck], k_f[:, ck]), _dot_nt(q_r[:, ck], k_r[:, ck]))
            av = _dot(jnp.concatenate([att.astype(BF16), k_l[:, ck].T], axis=0), vh)
            st = state_ref[h]
            o = av[:CHUNK] + _dot(q_i[:, ck], st.astype(BF16))
            dcol = jnp.transpose(jnp.broadcast_to(decay[:, ck], (dk, dk)))
            state_ref[h] = st * jnp.concatenate([dcol] * (dv // dk), axis=1) + av[CHUNK:]
            o = o * lax.rsqrt(jnp.mean(o * o, axis=-1, keepdims=True) + EPS) * gain
            og = og_ref[r, cv].astype(F32)
            o_ref[r, cv] = (o * (og * jax.nn.sigmoid(og))).astype(o_ref.dtype)


def _gla(proj, a_lr, w_up, b_up, head_gain, cast_weights, *, batch, seq, heads, dk, dv, cols, ts):
    m = proj.shape[0]
    ns = seq // ts
    dkt, dvt = heads * dk, heads * dv
    rank_pad = a_lr.shape[1]

    def rows(b, s):
        return b * ns + s

    c_in, c_out, c_shapes, c_vmem = _cast_plan(cast_weights, batch * ns, rows)
    kernel = functools.partial(_gla_kernel, heads=heads, dk=dk, dv=dv, ts=ts, n_cast=len(c_in))
    vmem = (2 * ts * (2 * dkt + 2 * dvt + rank_pad) * 2 + 2 * ts * dvt * 2 + heads * dk * dv * 4
            + c_vmem)
    return pl.pallas_call(
        kernel,
        out_shape=[jax.ShapeDtypeStruct((m, dvt), BF16)] + c_shapes,
        grid=(batch, ns),
        in_specs=[
            pl.BlockSpec((ts, dkt), lambda b, s: (rows(b, s), cols["q"] // dkt)),
            pl.BlockSpec((ts, dkt), lambda b, s: (rows(b, s), cols["k"] // dkt)),
            pl.BlockSpec((ts, dvt), lambda b, s: (rows(b, s), cols["v"] // dvt)),
            pl.BlockSpec((ts, dvt), lambda b, s: (rows(b, s), cols["og"] // dvt)),
            pl.BlockSpec((ts, rank_pad), lambda b, s: (rows(b, s), 0)),
            _resident(w_up.shape),
            _resident(b_up.shape),
            _resident(head_gain.shape),
        ] + c_in,
        out_specs=[pl.BlockSpec((ts, dvt), lambda b, s: (rows(b, s), 0))] + c_out,
        scratch_shapes=[pltpu.VMEM((heads, dk, dv), F32)],
        compiler_params=_params(("arbitrary", "arbitrary"), VMEM_LIMIT_CAP),
        name="gla",
    )(proj, proj, proj, proj, a_lr, w_up, b_up, head_gain, *cast_weights)


def _mix_out_kernel(*refs, tiles_per_seq, n_cast):
    (a_ref, ap_ref, o_ref, ga_ref, gb_ref, x_ref,
     cw_ref, wa_ref, wb_ref, wm_ref, g_ref, gnext_ref) = refs[:12]
    out_ref, hnext_ref = refs[12 + n_cast:14 + n_cast]
    _cast_slabs(refs[12:12 + n_cast], refs[14 + n_cast:])

    nc = wa_ref.shape[0]
    u = a_ref[:, 2 * nc:].astype(F32) * a_ref[:, :nc].astype(F32)
    u_prev = ap_ref[:, 2 * nc:].astype(F32) * ap_ref[:, :nc].astype(F32)
    first = (pl.program_id(0) % tiles_per_seq) == 0
    u_prev = jnp.where(first, 0.0, u_prev)
    p1 = u_prev[SUBLANES - 1:SUBLANES, :]
    p2 = u_prev[SUBLANES - 2:SUBLANES - 1, :]
    row = lax.broadcasted_iota(jnp.int32, u.shape, 0)
    u1 = jnp.where(row == 0, p1, pltpu.roll(u, 1, 0))
    u2 = jnp.where(row == 0, p2, jnp.where(row == 1, p1, pltpu.roll(u, 2, 0)))
    y = cw_ref[0] * u2 + cw_ref[1] * u1 + cw_ref[2] * u
    ca = (a_ref[:, nc:2 * nc].astype(F32) * y).astype(BF16)

    ya = _dot(ca, wa_ref[...])
    yb = _dot(o_ref[...], wb_ref[...])
    mix = jax.nn.sigmoid(ga_ref[...].astype(F32)) * ya + jax.nn.sigmoid(gb_ref[...].astype(F32)) * yb
    m = _dot(mix.astype(BF16), wm_ref[...])
    x_new = x_ref[...] + _rms(m, g_ref[...])
    out_ref[...] = x_new
    hnext_ref[...] = _rms(x_new, gnext_ref[...]).astype(hnext_ref.dtype)


def _mix_out(proj, o, x2, conv_w, wa, wb, wm, gain, gain_next, cast_weights, *, seq, cols, bm):
    m, d = x2.shape
    ca = conv_w.shape[-1]
    cb = o.shape[1]
    tiles_per_seq = seq // bm
    halo = bm // SUBLANES
    assert cols["a_b"] == cols["a_x"] + ca and cols["a_c"] == cols["a_x"] + 2 * ca
    assert cols["a_x"] % (3 * ca) == 0

    def prev(i):
        return jnp.maximum(i * halo - 1, 0)

    c_in, c_out, c_shapes, c_vmem = _cast_plan(cast_weights, m // bm, lambda i: i)
    kernel = functools.partial(_mix_out_kernel, tiles_per_seq=tiles_per_seq, n_cast=len(c_in))
    vmem = (2 * bm * (3 * ca + cb + 3 * d) * 2 + 4 * bm * d * 4
            + (ca + cb + d) * d * 2 + 4 * bm * d * 4 + c_vmem)
    return pl.pallas_call(
        kernel,
        out_shape=[jax.ShapeDtypeStruct((m, d), F32), jax.ShapeDtypeStruct((m, d), BF16)] + c_shapes,
        grid=(m // bm,),
        in_specs=[
            pl.BlockSpec((bm, 3 * ca), lambda i: (i, cols["a_x"] // (3 * ca))),
            pl.BlockSpec((SUBLANES, 3 * ca), lambda i: (prev(i), cols["a_x"] // (3 * ca))),
            pl.BlockSpec((bm, cb), lambda i: (i, 0)),
            pl.BlockSpec((bm, d), lambda i: (i, cols["gate_a"] // d)),
            pl.BlockSpec((bm, d), lambda i: (i, cols["gate_b"] // d)),
            pl.BlockSpec((bm, d), lambda i: (i, 0)),
            _resident(conv_w.shape),
            _resident(wa.shape),
            _resident(wb.shape),
            _resident(wm.shape),
            _resident(gain.shape),
            _resident(gain_next.shape),
        ] + c_in,
        out_specs=[pl.BlockSpec((bm, d), lambda i: (i, 0))] * 2 + c_out,
        compiler_params=_params(("arbitrary",), vmem + SPILL_ROOM),
        name="mix_out",
    )(proj, proj, o, proj, proj, x2, conv_w, wa, wb, wm, gain, gain_next, *cast_weights)


def _ffn_kernel(h_ref, wg_ref, wu_ref, wd_ref, gpost_ref, out_ref):
    j = pl.program_id(1)

    @pl.when(j == 0)
    def _():
        out_ref[...] = jnp.zeros_like(out_ref)

    h = h_ref[...]
    g = _dot(h, wg_ref[...])
    u = _dot(h, wu_ref[...])
    a = (g * jax.nn.sigmoid(g) * u).astype(BF16)
    out_ref[...] += _dot(a, wd_ref[...])

    @pl.when(j == pl.num_programs(1) - 1)
    def _():
        out_ref[...] = _rms(out_ref[...], gpost_ref[...])


def _ffn(h, wg, wu, wd, g_post, *, bm, bf):
    m, d = h.shape
    f = wg.shape[1]
    vmem = 2 * bm * d * 2 + 2 * bm * d * 4 + 2 * 3 * d * bf * 2 + 3 * bm * bf * 4
    return pl.pallas_call(
        _ffn_kernel,
        out_shape=jax.ShapeDtypeStruct((m, d), F32),
        grid=(m // bm, f // bf),
        in_specs=[
            pl.BlockSpec((bm, d), lambda i, j: (i, 0)),
            pl.BlockSpec((d, bf), lambda i, j: (0, j)),
            pl.BlockSpec((d, bf), lambda i, j: (0, j)),
            pl.BlockSpec((bf, d), lambda i, j: (j, 0)),
            pl.BlockSpec((1, d), lambda i, j: (0, 0)),
        ],
        out_specs=pl.BlockSpec((bm, d), lambda i, j: (i, 0)),
        compiler_params=_params(("parallel", "arbitrary"), vmem + SPILL_ROOM),
        name="ffn",
    )(h, wg, wu, wd, g_post)


def _ple_kernel(x_ref, f_ref, p_ref, gpre_ref, wpg_ref, wpp_ref, gpost_ref, out_ref):
    x = x_ref[...] + f_ref[...]
    h = _rms(x, gpre_ref[...]).astype(BF16)
    e = jax.nn.sigmoid(_dot(h, wpg_ref[...])) * _dot(p_ref[...].astype(BF16),
                                                      wpp_ref[...].astype(BF16))
    out_ref[...] = x + _rms(e, gpost_ref[...])


def _ple(x2, f2, p2, g_pre, wpg, wpp, g_post, *, bm):
    m, d = x2.shape
    pd = p2.shape[1]
    vmem = 6 * bm * d * 4 + 2 * bm * pd * 4 + d * d * 2 + pd * d * 4 + 4 * bm * d * 4
    return pl.pallas_call(
        _ple_kernel,
        out_shape=jax.ShapeDtypeStruct((m, d), F32),
        grid=(m // bm,),
        in_specs=[
            pl.BlockSpec((bm, d), lambda i: (i, 0)),
            pl.BlockSpec((bm, d), lambda i: (i, 0)),
            pl.BlockSpec((bm, pd), lambda i: (i, 0)),
            _resident(g_pre.shape),
            _resident(wpg.shape),
            _resident(wpp.shape),
            _resident(g_post.shape),
        ],
        out_specs=pl.BlockSpec((bm, d), lambda i: (i, 0)),
        compiler_params=_params(("parallel",), vmem + SPILL_ROOM),
        name="ple",
    )(x2, f2, p2, g_pre, wpg, wpp, g_post)


def _tiles(m, seq, n_proj, d_ff):
    tiles = dict(proj_rows=1024, proj_cols=1024, gla_rows=256, mix_rows=256,
                 ffn_rows=1024, ffn_cols=512, ple_rows=512)
    assert all(m % tiles[k] == 0 for k in ("proj_rows", "mix_rows", "ffn_rows", "ple_rows"))
    assert seq % tiles["gla_rows"] == 0 and tiles["gla_rows"] % CHUNK == 0
    assert seq % tiles["mix_rows"] == 0 and tiles["mix_rows"] % SUBLANES == 0
    assert n_proj % tiles["proj_cols"] == 0 and d_ff % tiles["ffn_cols"] == 0
    return tiles


def _proj_layout(mix_a, dk_total, mix_b, d):
    names = ["a_x", "a_b", "a_c", "q", "k", "v", "og", "gate_a", "gate_b"]
    widths = [mix_a, mix_a, mix_a, dk_total, dk_total, mix_b, mix_b, d, d]
    cols, off = {}, 0
    for nme, wdt in zip(names, widths):
        assert off % wdt == 0, (nme, off, wdt)
        cols[nme] = off
        off += wdt
    return cols, cols["gate_a"]


def kernel(x, p, w_in, conv_w, w_a_out, w_alpha_up, b_alpha_up, gla_head_gain, w_b_out, w_mix_out, g_pre_mix, g_post_mix, g_pre_ffn, g_post_ffn, w_ff_gate, w_ff_up, w_ff_down, g_pre_ple, g_post_ple, w_ple_gate, w_ple_proj):
    batch, seq, d = x.shape
    depth = w_in.shape[0]
    mix_a = conv_w.shape[-1]
    rank, dk_total = w_alpha_up.shape[1], w_alpha_up.shape[2]
    dv = gla_head_gain.shape[-1]
    mix_b = w_b_out.shape[1]
    heads = mix_b // dv
    dk = dk_total // heads
    assert rank <= LANES and seq % CHUNK == 0 and conv_w.shape[1] == CONV_WIDTH
    cols, lr0 = _proj_layout(mix_a, dk_total, mix_b, d)
    t = _tiles(batch * seq, seq, w_in.shape[2] - rank, w_ff_gate.shape[2])

    xs = x.reshape(batch * seq, d)
    for i in range(depth):
        proj, a_lr = _norm_proj(xs, g_pre_mix[i][None], w_in[i].T, lr0=lr0, rank=rank,
                                bm=t["proj_rows"], bn=t["proj_cols"])
        o, wa, wb, wm, wg = _gla(
            proj, a_lr, w_alpha_up[i], b_alpha_up[i][None], gla_head_gain[i][None],
            [w_a_out[i], w_b_out[i], w_mix_out[i], w_ff_gate[i]],
            batch=batch, seq=seq, heads=heads, dk=dk, dv=dv, cols=cols, ts=t["gla_rows"])
        xs, h_ffn, wu, wd, wpg = _mix_out(
            proj, o, xs, conv_w[i][:, None, :], wa, wb, wm, g_post_mix[i][None],
            g_pre_ffn[i][None],
            [w_ff_up[i], w_ff_down[i], w_ple_gate[i]], seq=seq, cols=cols, bm=t["mix_rows"])
        f_norm = _ffn(h_ffn, wg, wu, wd, g_post_ffn[i][None],
                      bm=t["ffn_rows"], bf=t["ffn_cols"])
        xs = _ple(xs, f_norm, p[i].reshape(batch * seq, -1), g_pre_ple[i][None], wpg,
                  w_ple_proj[i], g_post_ple[i][None], bm=t["ple_rows"])
    return xs.reshape(batch, seq, d)
```

```python
import functools

import jax
import jax.numpy as jnp
from jax import lax
from jax.experimental import pallas as pl
from jax.experimental.pallas import tpu as pltpu

EPS = 1e-6
CHUNK = 64
GLA_TAU = 16.0
CONV_WIDTH = 3
LANES = 128
SUBLANES = 8
VMEM_LIMIT_CAP = 60000 * 1024
SPILL_ROOM = 8 << 20

F32 = jnp.float32
BF16 = jnp.bfloat16


def _rms(x, gain):
    ms = jnp.mean(x * x, axis=-1, keepdims=True)
    return x * lax.rsqrt(ms + EPS) * gain


def _dot(a, b):
    return jnp.dot(a, b, preferred_element_type=F32)


def _dot_nt(a, b):
    return lax.dot_general(a, b, (((1,), (1,)), ((), ())), preferred_element_type=F32)


def _params(semantics, vmem_bytes):
    return pltpu.CompilerParams(
        dimension_semantics=semantics,
        vmem_limit_bytes=min(int(vmem_bytes), VMEM_LIMIT_CAP),
    )


def _resident(shape):
    return pl.BlockSpec(shape, lambda *_: (0,) * len(shape), pipeline_mode=pl.Buffered(1))


BF16_ROWS = 16


def _cast_plan(weights, n_slabs, slab_of):
    in_specs, out_specs, out_shapes, vmem = [], [], [], 0
    for w in weights:
        rows, width = w.shape
        assert rows % (n_slabs * BF16_ROWS) == 0, (w.shape, n_slabs)
        slab = (rows // n_slabs, width)
        spec = pl.BlockSpec(slab, lambda *idx: (slab_of(*idx), 0))
        in_specs.append(spec)
        out_specs.append(spec)
        out_shapes.append(jax.ShapeDtypeStruct(w.shape, BF16))
        vmem += 2 * slab[0] * slab[1] * (4 + 2)
    return in_specs, out_specs, out_shapes, vmem


def _cast_slabs(src_refs, dst_refs):
    for src, dst in zip(src_refs, dst_refs):
        dst[...] = src[...].astype(dst.dtype)


X_PARTS = 4


def _norm_proj_kernel(*refs):
    x_refs = refs[:X_PARTS]
    g_ref, wt_ref, wlr_ref, o_ref, lr_ref, h_ref = refs[X_PARTS:]

    @pl.when(pl.program_id(1) == 0)
    def _():
        rows = h_ref.shape[0] // X_PARTS
        for k, r in enumerate(x_refs):
            h_ref[k * rows:(k + 1) * rows, :] = _rms(r[...], g_ref[...]).astype(BF16)
        wlr = wlr_ref[...]
        wlr = jnp.concatenate([wlr, jnp.zeros((LANES - wlr.shape[0], wlr.shape[1]), F32)], axis=0)
        lr_ref[...] = _dot_nt(h_ref[...], wlr.astype(BF16)).astype(lr_ref.dtype)

    o_ref[...] = _dot_nt(h_ref[...], wt_ref[...].astype(BF16)).astype(o_ref.dtype)


def _norm_proj(x2, gain, wt, *, lr0, rank, bm, bn):
    m, d = x2.shape
    n = wt.shape[0] - rank
    assert lr0 % bn == 0 and n % bn == 0 and rank % SUBLANES == 0

    def w_rows(i, j):
        skip = jnp.where(j * bn >= lr0, rank // SUBLANES, 0)
        return ((j * (bn // SUBLANES) + skip) * SUBLANES, 0)

    ni, nj = m // bm, n // bn
    switch = [(k + 1) * nj // (X_PARTS + 1) for k in range(X_PARTS)]
    assert bm % (X_PARTS * BF16_ROWS) == 0 and all(0 < s < nj for s in switch)

    def x_part(k):
        def index(i, j):
            tile = jnp.minimum(i + jnp.where(j >= switch[k], 1, 0), ni - 1)
            return (tile * X_PARTS + k, 0)
        return pl.BlockSpec((bm // X_PARTS, d), index)

    vmem = 2 * bm * d * 4 + 2 * bn * d * 4 + 2 * bm * bn * 2 + bm * d * 2 + bm * bn * 4 + d * bn * 2
    return pl.pallas_call(
        _norm_proj_kernel,
        out_shape=(jax.ShapeDtypeStruct((m, n), BF16), jax.ShapeDtypeStruct((m, LANES), BF16)),
        grid=(ni, nj),
        in_specs=[x_part(k) for k in range(X_PARTS)] + [
            pl.BlockSpec((1, d), lambda i, j: (0, 0)),
            pl.BlockSpec((pl.Element(bn), pl.Element(d)), w_rows),
            pl.BlockSpec((pl.Element(rank), pl.Element(d)), lambda i, j: (lr0, 0)),
        ],
        out_specs=(pl.BlockSpec((bm, bn), lambda i, j: (i, j)),
                   pl.BlockSpec((bm, LANES), lambda i, j: (i, 0))),
        scratch_shapes=[pltpu.VMEM((bm, d), BF16)],
        compiler_params=_params(("arbitrary", "arbitrary"), vmem + SPILL_ROOM),
        name="norm_proj",
    )(*([x2] * X_PARTS), gain, wt, wt)


def _gla_kernel(*refs, heads, dk, dv, ts, n_cast):
    q_ref, k_ref, v_ref, og_ref, alr_ref, wup_ref, bup_ref, hg_ref = refs[:8]
    o_ref = refs[8 + n_cast]
    state_ref = refs[-1]
    _cast_slabs(refs[8:8 + n_cast], refs[9 + n_cast:9 + 2 * n_cast])

    @pl.when(pl.program_id(1) == 0)
    def _():
        state_ref[...] = jnp.zeros_like(state_ref)

    w_up = wup_ref[...]
    rank = w_up.shape[0]
    w_up = jnp.concatenate([w_up, jnp.zeros((LANES - rank, w_up.shape[1]), F32)], axis=0)
    z = _dot(alr_ref[...], w_up.astype(BF16)) + bup_ref[...]
    log_a = (jnp.minimum(z, 0.0) - jnp.log(1.0 + jnp.exp(-jnp.abs(z)))) * (1.0 / GLA_TAU)

    row = lax.broadcasted_iota(jnp.int32, (ts, ts), 0)
    col = lax.broadcasted_iota(jnp.int32, (ts, ts), 1)
    tri = ((row // CHUNK == col // CHUNK) & (col <= row)).astype(BF16)
    la_hi = log_a.astype(BF16)
    la_lo = (log_a - la_hi.astype(F32)).astype(BF16)
    b = _dot(tri, la_hi) + _dot(tri, la_lo)

    crow = lax.broadcasted_iota(jnp.int32, (CHUNK, CHUNK), 0)
    ccol = lax.broadcasted_iota(jnp.int32, (CHUNK, CHUNK), 1)
    lower = ccol <= crow

    scale = dk ** -0.5
    gain = hg_ref[...]
    for c in range(ts // CHUNK):
        r = slice(c * CHUNK, (c + 1) * CHUNK)
        bc = b[r, :]
        mid = bc[CHUNK // 2:CHUNK // 2 + 1, :]
        last = bc[CHUNK - 1:CHUNK, :]
        e_fwd = jnp.exp(bc - mid)
        e_rev = jnp.exp(mid - bc)
        qc = q_ref[r, :].astype(F32) * scale
        kc = k_ref[r, :].astype(F32)
        qe = qc * e_fwd
        ke = kc * e_rev
        q_f = qe.astype(BF16)
        k_f = ke.astype(BF16)
        q_r = (qc * e_rev).astype(BF16)
        k_r = (kc * e_fwd).astype(BF16)
        k_l = (ke * jnp.exp(last - mid)).astype(BF16)
        q_i = (qe * jnp.exp(mid)).astype(BF16)
        decay = jnp.exp(last)
        for h in range(heads):
            ck = slice(h * dk, (h + 1) * dk)
            cv = slice(h * dv, (h + 1) * dv)
            vh = v_ref[r, cv]
            att = jnp.where(lower, _dot_nt(q_f[:, ck], k_f[:, ck]), _dot_nt(q_r[:, ck], k_r[:, ck]))
            av = _dot(jnp.concatenate([att.astype(BF16), k_l[:, ck].T], axis=0), vh)
            st = state_ref[h]
            o = av[:CHUNK] + _dot(q_i[:, ck], st.astype(BF16))
            dcol = jnp.transpose(jnp.broadcast_to(decay[:, ck], (dk, dk)))
            state_ref[h] = st * jnp.concatenate([dcol] * (dv // dk), axis=1) + av[CHUNK:]
            o = o * lax.rsqrt(jnp.mean(o * o, axis=-1, keepdims=True) + EPS) * gain
            og = og_ref[r, cv].astype(F32)
            o_ref[r, cv] = (o * (og * jax.nn.sigmoid(og))).astype(o_ref.dtype)


def _gla(proj, a_lr, w_up, b_up, head_gain, cast_weights, *, batch, seq, heads, dk, dv, cols, ts):
    m = proj.shape[0]
    ns = seq // ts
    dkt, dvt = heads * dk, heads * dv
    rank_pad = a_lr.shape[1]

    def rows(b, s):
        return b * ns + s

    c_in, c_out, c_shapes, c_vmem = _cast_plan(cast_weights, batch * ns, rows)
    kernel = functools.partial(_gla_kernel, heads=heads, dk=dk, dv=dv, ts=ts, n_cast=len(c_in))
    vmem = (2 * ts * (2 * dkt + 2 * dvt + rank_pad) * 2 + 2 * ts * dvt * 2 + heads * dk * dv * 4
            + c_vmem)
    return pl.pallas_call(
        kernel,
        out_shape=[jax.ShapeDtypeStruct((m, dvt), BF16)] + c_shapes,
        grid=(batch, ns),
        in_specs=[
            pl.BlockSpec((ts, dkt), lambda b, s: (rows(b, s), cols["q"] // dkt)),
            pl.BlockSpec((ts, dkt), lambda b, s: (rows(b, s), cols["k"] // dkt)),
            pl.BlockSpec((ts, dvt), lambda b, s: (rows(b, s), cols["v"] // dvt)),
            pl.BlockSpec((ts, dvt), lambda b, s: (rows(b, s), cols["og"] // dvt)),
            pl.BlockSpec((ts, rank_pad), lambda b, s: (rows(b, s), 0)),
            _resident(w_up.shape),
            _resident(b_up.shape),
            _resident(head_gain.shape),
        ] + c_in,
        out_specs=[pl.BlockSpec((ts, dvt), lambda b, s: (rows(b, s), 0))] + c_out,
        scratch_shapes=[pltpu.VMEM((heads, dk, dv), F32)],
        compiler_params=_params(("arbitrary", "arbitrary"), VMEM_LIMIT_CAP),
        name="gla",
    )(proj, proj, proj, proj, a_lr, w_up, b_up, head_gain, *cast_weights)


def _mix_out_kernel(*refs, tiles_per_seq, n_cast):
    (a_ref, ap_ref, o_ref, ga_ref, gb_ref, x_ref,
     cw_ref, wa_ref, wb_ref, wm_ref, g_ref, gnext_ref) = refs[:12]
    out_ref, hnext_ref = refs[12 + n_cast:14 + n_cast]
    _cast_slabs(refs[12:12 + n_cast], refs[14 + n_cast:])

    nc = wa_ref.shape[0]
    u = a_ref[:, 2 * nc:].astype(F32) * a_ref[:, :nc].astype(F32)
    u_prev = ap_ref[:, 2 * nc:].astype(F32) * ap_ref[:, :nc].astype(F32)
    first = (pl.program_id(0) % tiles_per_seq) == 0
    u_prev = jnp.where(first, 0.0, u_prev)
    p1 = u_prev[SUBLANES - 1:SUBLANES, :]
    p2 = u_prev[SUBLANES - 2:SUBLANES - 1, :]
    row = lax.broadcasted_iota(jnp.int32, u.shape, 0)
    u1 = jnp.where(row == 0, p1, pltpu.roll(u, 1, 0))
    u2 = jnp.where(row == 0, p2, jnp.where(row == 1, p1, pltpu.roll(u, 2, 0)))
    y = cw_ref[0] * u2 + cw_ref[1] * u1 + cw_ref[2] * u
    ca = (a_ref[:, nc:2 * nc].astype(F32) * y).astype(BF16)

    ya = _dot(ca, wa_ref[...])
    yb = _dot(o_ref[...], wb_ref[...])
    mix = jax.nn.sigmoid(ga_ref[...].astype(F32)) * ya + jax.nn.sigmoid(gb_ref[...].astype(F32)) * yb
    m = _dot(mix.astype(BF16), wm_ref[...])
    x_new = x_ref[...] + _rms(m, g_ref[...])
    out_ref[...] = x_new
    hnext_ref[...] = _rms(x_new, gnext_ref[...]).astype(hnext_ref.dtype)


def _mix_out(proj, o, x2, conv_w, wa, wb, wm, gain, gain_next, cast_weights, *, seq, cols, bm):
    m, d = x2.shape
    ca = conv_w.shape[-1]
    cb = o.shape[1]
    tiles_per_seq = seq // bm
    halo = bm // SUBLANES
    assert cols["a_b"] == cols["a_x"] + ca and cols["a_c"] == cols["a_x"] + 2 * ca
    assert cols["a_x"] % (3 * ca) == 0

    def prev(i):
        return jnp.maximum(i * halo - 1, 0)

    c_in, c_out, c_shapes, c_vmem = _cast_plan(cast_weights, m // bm, lambda i: i)
    kernel = functools.partial(_mix_out_kernel, tiles_per_seq=tiles_per_seq, n_cast=len(c_in))
    vmem = (2 * bm * (3 * ca + cb + 3 * d) * 2 + 4 * bm * d * 4
            + (ca + cb + d) * d * 2 + 4 * bm * d * 4 + c_vmem)
    return pl.pallas_call(
        kernel,
        out_shape=[jax.ShapeDtypeStruct((m, d), F32), jax.ShapeDtypeStruct((m, d), BF16)] + c_shapes,
        grid=(m // bm,),
        in_specs=[
            pl.BlockSpec((bm, 3 * ca), lambda i: (i, cols["a_x"] // (3 * ca))),
            pl.BlockSpec((SUBLANES, 3 * ca), lambda i: (prev(i), cols["a_x"] // (3 * ca))),
            pl.BlockSpec((bm, cb), lambda i: (i, 0)),
            pl.BlockSpec((bm, d), lambda i: (i, cols["gate_a"] // d)),
            pl.BlockSpec((bm, d), lambda i: (i, cols["gate_b"] // d)),
            pl.BlockSpec((bm, d), lambda i: (i, 0)),
            _resident(conv_w.shape),
            _resident(wa.shape),
            _resident(wb.shape),
            _resident(wm.shape),
            _resident(gain.shape),
            _resident(gain_next.shape),
        ] + c_in,
        out_specs=[pl.BlockSpec((bm, d), lambda i: (i, 0))] * 2 + c_out,
        compiler_params=_params(("arbitrary",), vmem + SPILL_ROOM),
        name="mix_out",
    )(proj, proj, o, proj, proj, x2, conv_w, wa, wb, wm, gain, gain_next, *cast_weights)


def _ffn_kernel(h_ref, wg_ref, wu_ref, wd_ref, gpost_ref, out_ref):
    j = pl.program_id(1)

    @pl.when(j == 0)
    def _():
        out_ref[...] = jnp.zeros_like(out_ref)

    h = h_ref[...]
    g = _dot(h, wg_ref[...])
    u = _dot(h, wu_ref[...])
    a = (g * jax.nn.sigmoid(g) * u).astype(BF16)
    out_ref[...] += _dot(a, wd_ref[...])

    @pl.when(j == pl.num_programs(1) - 1)
    def _():
        out_ref[...] = _rms(out_ref[...], gpost_ref[...])


def _ffn(h, wg, wu, wd, g_post, *, bm, bf):
    m, d = h.shape
    f = wg.shape[1]
    vmem = 2 * bm * d * 2 + 2 * bm * d * 4 + 2 * 3 * d * bf * 2 + 3 * bm * bf * 4
    nj = f // bf

    def blk(i, j):
        return jnp.where(i % 2 == 0, j, nj - 1 - j)

    return pl.pallas_call(
        _ffn_kernel,
        out_shape=jax.ShapeDtypeStruct((m, d), F32),
        grid=(m // bm, nj),
        in_specs=[
            pl.BlockSpec((bm, d), lambda i, j: (i, 0)),
            pl.BlockSpec((d, bf), lambda i, j: (0, blk(i, j))),
            pl.BlockSpec((d, bf), lambda i, j: (0, blk(i, j))),
            pl.BlockSpec((bf, d), lambda i, j: (blk(i, j), 0)),
            pl.BlockSpec((1, d), lambda i, j: (0, 0)),
        ],
        out_specs=pl.BlockSpec((bm, d), lambda i, j: (i, 0)),
        compiler_params=_params(("parallel", "arbitrary"), vmem + SPILL_ROOM),
        name="ffn",
    )(h, wg, wu, wd, g_post)


def _ple_kernel(x_ref, f_ref, p_ref, gpre_ref, wpg_ref, wpp_ref, gpost_ref, out_ref):
    x = x_ref[...] + f_ref[...]
    h = _rms(x, gpre_ref[...]).astype(BF16)
    e = jax.nn.sigmoid(_dot(h, wpg_ref[...])) * _dot(p_ref[...].astype(BF16),
                                                      wpp_ref[...].astype(BF16))
    out_ref[...] = x + _rms(e, gpost_ref[...])


def _ple(x2, f2, p2, g_pre, wpg, wpp, g_post, *, bm):
    m, d = x2.shape
    pd = p2.shape[1]
    vmem = 6 * bm * d * 4 + 2 * bm * pd * 4 + d * d * 2 + pd * d * 4 + 4 * bm * d * 4
    return pl.pallas_call(
        _ple_kernel,
        out_shape=jax.ShapeDtypeStruct((m, d), F32),
        grid=(m // bm,),
        in_specs=[
            pl.BlockSpec((bm, d), lambda i: (i, 0)),
            pl.BlockSpec((bm, d), lambda i: (i, 0)),
            pl.BlockSpec((bm, pd), lambda i: (i, 0)),
            _resident(g_pre.shape),
            _resident(wpg.shape),
            _resident(wpp.shape),
            _resident(g_post.shape),
        ],
        out_specs=pl.BlockSpec((bm, d), lambda i: (i, 0)),
        compiler_params=_params(("parallel",), vmem + SPILL_ROOM),
        name="ple",
    )(x2, f2, p2, g_pre, wpg, wpp, g_post)


def _tiles(m, seq, n_proj, d_ff):
    tiles = dict(proj_rows=1024, proj_cols=1024, gla_rows=256, mix_rows=256,
                 ffn_rows=1024, ffn_cols=512, ple_rows=512)
    assert all(m % tiles[k] == 0 for k in ("proj_rows", "mix_rows", "ffn_rows", "ple_rows"))
    assert seq % tiles["gla_rows"] == 0 and tiles["gla_rows"] % CHUNK == 0
    assert seq % tiles["mix_rows"] == 0 and tiles["mix_rows"] % SUBLANES == 0
    assert n_proj % tiles["proj_cols"] == 0 and d_ff % tiles["ffn_cols"] == 0
    return tiles


def _proj_layout(mix_a, dk_total, mix_b, d):
    names = ["a_x", "a_b", "a_c", "q", "k", "v", "og", "gate_a", "gate_b"]
    widths = [mix_a, mix_a, mix_a, dk_total, dk_total, mix_b, mix_b, d, d]
    cols, off = {}, 0
    for nme, wdt in zip(names, widths):
        assert off % wdt == 0, (nme, off, wdt)
        cols[nme] = off
        off += wdt
    return cols, cols["gate_a"]


def kernel(x, p, w_in, conv_w, w_a_out, w_alpha_up, b_alpha_up, gla_head_gain, w_b_out, w_mix_out, g_pre_mix, g_post_mix, g_pre_ffn, g_post_ffn, w_ff_gate, w_ff_up, w_ff_down, g_pre_ple, g_post_ple, w_ple_gate, w_ple_proj):
    batch, seq, d = x.shape
    depth = w_in.shape[0]
    mix_a = conv_w.shape[-1]
    rank, dk_total = w_alpha_up.shape[1], w_alpha_up.shape[2]
    dv = gla_head_gain.shape[-1]
    mix_b = w_b_out.shape[1]
    heads = mix_b // dv
    dk = dk_total // heads
    assert rank <= LANES and seq % CHUNK == 0 and conv_w.shape[1] == CONV_WIDTH
    cols, lr0 = _proj_layout(mix_a, dk_total, mix_b, d)
    t = _tiles(batch * seq, seq, w_in.shape[2] - rank, w_ff_gate.shape[2])

    xs = x.reshape(batch * seq, d)
    for i in range(depth):
        proj, a_lr = _norm_proj(xs, g_pre_mix[i][None], w_in[i].T, lr0=lr0, rank=rank,
                                bm=t["proj_rows"], bn=t["proj_cols"])
        o, wa, wb, wm, wg = _gla(
            proj, a_lr, w_alpha_up[i], b_alpha_up[i][None], gla_head_gain[i][None],
            [w_a_out[i], w_b_out[i], w_mix_out[i], w_ff_gate[i]],
            batch=batch, seq=seq, heads=heads, dk=dk, dv=dv, cols=cols, ts=t["gla_rows"])
        xs, h_ffn, wu, wd, wpg = _mix_out(
            proj, o, xs, conv_w[i][:, None, :], wa, wb, wm, g_post_mix[i][None],
            g_pre_ffn[i][None],
            [w_ff_up[i], w_ff_down[i], w_ple_gate[i]], seq=seq, cols=cols, bm=t["mix_rows"])
        f_norm = _ffn(h_ffn, wg, wu, wd, g_post_ffn[i][None],
                      bm=t["ffn_rows"], bf=t["ffn_cols"])
        xs = _ple(xs, f_norm, p[i].reshape(batch * seq, -1), g_pre_ple[i][None], wpg,
                  w_ple_proj[i], g_post_ple[i][None], bm=t["ple_rows"])
    return xs.reshape(batch, seq, d)
```
